```python
import jax, jax.numpy as jnp
from jax import lax
import numpy as np

D_MODEL = 4096
BATCH = 1
SEQ = 16384
DEPTH = 1

N_ATTN_HEADS = 16
HEAD_DIM = 128
D_ATTN = N_ATTN_HEADS * HEAD_DIM
MOBA_BLOCK = 256
MOBA_TOPK = 3
QUERY_CHUNK = 64
ROPE_THETA = 10000.0
D_CONV = 2048
CONV_GROUPS = 16
CONV_WIDTH = 3
D_FF = 11008
FFN_CONV_WIDTH = 3
D_PLE = 256
N_BRANCHES = 2
RMS_EPS = 1e-6
IN_SPLITS = (D_ATTN, D_ATTN, D_ATTN, D_CONV, D_CONV, D_CONV, D_MODEL, D_MODEL)
D_IN_PROJ = sum(IN_SPLITS)

kernel_name = "hybrid_moba_shortconv_convffn_block"


def rms_norm(x, g):
    xf = x.astype(jnp.float32)
    y = xf * lax.rsqrt(jnp.mean(xf * xf, axis=-1, keepdims=True) + RMS_EPS)
    return (y * g.astype(jnp.float32)).astype(x.dtype)


def causal_dwconv(x, w):
    width = w.shape[0]
    s = x.shape[1]
    xp = jnp.pad(x, ((0, 0), (width - 1, 0), (0, 0)))
    y = w[width - 1] * x
    for j in range(width - 1):
        y = y + w[j] * xp[:, j:j + s]
    return y


def rope(x, positions):
    half = x.shape[-1] // 2
    inv_freq = ROPE_THETA ** (-jnp.arange(half, dtype=jnp.float32) / half)
    ang = positions.astype(jnp.float32)[:, None, :, None] * inv_freq
    cos, sin = jnp.cos(ang), jnp.sin(ang)
    xf = x.astype(jnp.float32)
    x1, x2 = xf[..., :half], xf[..., half:]
    return jnp.concatenate([x1 * cos - x2 * sin, x2 * cos + x1 * sin], axis=-1).astype(x.dtype)


def moba_attention(q, k, v):
    b, h, s, d = q.shape
    nb = -(-s // MOBA_BLOCK)
    s_pad = nb * MOBA_BLOCK
    topk = min(MOBA_TOPK, nb)
    pad = ((0, 0), (0, 0), (0, s_pad - s), (0, 0))
    kp = jnp.pad(k, pad)
    vp = jnp.pad(v, pad)
    k_blocks = kp.reshape(b, h, nb, MOBA_BLOCK, d)
    v_blocks = vp.reshape(b, h, nb, MOBA_BLOCK, d)
    k_mean = jnp.mean(k_blocks.astype(jnp.float32), axis=3).astype(k.dtype)
    scale = d ** -0.5
    n_chunks = s // QUERY_CHUNK
    b_idx = jnp.arange(b)[:, None, None, None]
    h_idx = jnp.arange(h)[None, :, None, None]
    blk_ids = jnp.arange(nb)

    def chunk(c):
        q0 = c * QUERY_CHUNK
        qb = q0 // MOBA_BLOCK
        qc = lax.dynamic_slice_in_dim(q, q0, QUERY_CHUNK, axis=2)
        qpos = q0 + jnp.arange(QUERY_CHUNK)
        gate = jnp.einsum('bhqd,bhnd->bhqn', qc, k_mean).astype(jnp.float32)
        gate = jnp.where(blk_ids < qb, gate, -jnp.inf)
        _, idx = lax.top_k(gate, topk)
        sel_valid = idx < qb
        k_sel = k_blocks[b_idx, h_idx, idx]
        v_sel = v_blocks[b_idx, h_idx, idx]
        s_sel = jnp.einsum('bhqd,bhqjkd->bhqjk', qc, k_sel).astype(jnp.float32) * scale
        s_sel = jnp.where(sel_valid[..., None], s_sel, -jnp.inf)
        s_sel = s_sel.reshape(b, h, QUERY_CHUNK, topk * MOBA_BLOCK)
        own0 = qb * MOBA_BLOCK
        k_own = lax.dynamic_slice_in_dim(kp, own0, MOBA_BLOCK, axis=2)
        v_own = lax.dynamic_slice_in_dim(vp, own0, MOBA_BLOCK, axis=2)
        s_own = jnp.einsum('bhqd,bhkd->bhqk', qc, k_own).astype(jnp.float32) * scale
        kpos = own0 + jnp.arange(MOBA_BLOCK)
        s_own = jnp.where(kpos[None, :] <= qpos[:, None], s_own, -jnp.inf)
        probs = jax.nn.softmax(jnp.concatenate([s_sel, s_own], axis=-1), axis=-1)
        p_sel = probs[..., :topk * MOBA_BLOCK].reshape(b, h, QUERY_CHUNK, topk, MOBA_BLOCK).astype(v.dtype)
        p_own = probs[..., topk * MOBA_BLOCK:].astype(v.dtype)
        return (jnp.einsum('bhqjk,bhqjkd->bhqd', p_sel, v_sel)
                + jnp.einsum('bhqk,bhkd->bhqd', p_own, v_own))

    outs = lax.map(chunk, jnp.arange(n_chunks))
    return outs.transpose(1, 2, 0, 3, 4).reshape(b, h, s, d)


def setup_inputs(seed: int = 0) -> dict:
    key = jax.random.key(seed)
    ks = jax.random.split(key, 17)
    f32 = jnp.float32

    def nrm(k, shape, fan_in):
        return jax.random.normal(k, shape, f32) * (fan_in ** -0.5)

    def gain(k, shape):
        return 1.0 + 0.05 * jax.random.normal(k, shape, f32)

    return {
        "x": jax.random.normal(ks[0], (BATCH, SEQ, D_MODEL), f32),
        "p": jax.random.normal(ks[1], (DEPTH, BATCH, SEQ, D_PLE), f32),
        "positions": jnp.broadcast_to(jnp.arange(SEQ, dtype=jnp.int32), (BATCH, SEQ)),
        "w_norm_mix": gain(ks[2], (DEPTH, D_MODEL)),
        "w_in": nrm(ks[3], (DEPTH, D_MODEL, D_IN_PROJ), D_MODEL),
        "q_norm": gain(ks[4], (DEPTH, HEAD_DIM)),
        "k_norm": gain(ks[5], (DEPTH, HEAD_DIM)),
        "conv_mix_w": nrm(ks[6], (DEPTH, CONV_WIDTH, D_CONV), CONV_WIDTH),
        "w_attn_branch": nrm(ks[7], (DEPTH, D_ATTN, D_MODEL), D_ATTN),
        "w_conv_branch": nrm(ks[8], (DEPTH, D_CONV, D_MODEL), D_CONV),
        "w_out": nrm(ks[9], (DEPTH, D_MODEL, D_MODEL), D_MODEL),
        "w_norm_ffn": gain(ks[10], (DEPTH, D_MODEL)),
        "w_up": nrm(ks[11], (DEPTH, D_MODEL, 2 * D_FF), D_MODEL),
        "ffn_conv_w": nrm(ks[12], (DEPTH, FFN_CONV_WIDTH, 2 * D_FF), FFN_CONV_WIDTH),
        "w_down": nrm(ks[13], (DEPTH, D_FF, D_MODEL), D_FF),
        "w_ple": nrm(ks[14], (DEPTH, D_PLE, D_MODEL), D_PLE),
        "w_ple_gate": nrm(ks[15], (DEPTH, D_MODEL, D_MODEL), D_MODEL),
    }


def reference(x, p, positions, w_norm_mix, w_in, q_norm, k_norm, conv_mix_w,
              w_attn_branch, w_conv_branch, w_out, w_norm_ffn, w_up, ffn_conv_w,
              w_down, w_ple, w_ple_gate):
    b, s, _ = x.shape
    offsets = [int(o) for o in np.cumsum(IN_SPLITS)[:-1]]
    for i in range(DEPTH):
        h = rms_norm(x, w_norm_mix[i])
        z = h @ w_in[i]
        q, k, v, cb, cc, cx, g_attn, g_conv = jnp.split(z, offsets, axis=-1)

        def heads(t):
            return t.reshape(b, s, N_ATTN_HEADS, HEAD_DIM).transpose(0, 2, 1, 3)

        qh = rope(rms_norm(heads(q), q_norm[i]), positions)
        kh = rope(rms_norm(heads(k), k_norm[i]), positions)
        vh = heads(v)
        attn = moba_attention(qh, kh, vh).transpose(0, 2, 1, 3).reshape(b, s, D_ATTN)

        conv_out = cb * causal_dwconv(cc * cx, conv_mix_w[i])

        merged = (jax.nn.sigmoid(g_attn) * (attn @ w_attn_branch[i])
                  + jax.nn.sigmoid(g_conv) * (conv_out @ w_conv_branch[i]))
        x = x + merged @ w_out[i]

        h2 = rms_norm(x, w_norm_ffn[i])
        u = causal_dwconv(h2 @ w_up[i], ffn_conv_w[i])
        u_g, u_v = jnp.split(u, 2, axis=-1)
        x = x + (jax.nn.silu(u_g) * u_v) @ w_down[i]

        x = x + jax.nn.sigmoid(x @ w_ple_gate[i]) * (p[i] @ w_ple[i])
    return x
```

```python
import functools
import math

import jax
import jax.numpy as jnp
from jax import lax
from jax.experimental import pallas as pl
from jax.experimental.pallas import tpu as pltpu

N_ATTN_HEADS = 16
HEAD_DIM = 128
MOBA_BLOCK = 256
MOBA_TOPK = 3
ROPE_THETA = 10000.0
RMS_EPS = 1e-6
D_ATTN = N_ATTN_HEADS * HEAD_DIM
D_CONV = 2048

LANES = 128
MASK_VALUE = -1e30
VMEM_LIMIT_BYTES = 56 * 1024 * 1024
BF16 = jnp.bfloat16
F32 = jnp.float32


def _params(*semantics):
    return pltpu.CompilerParams(dimension_semantics=semantics,
                                vmem_limit_bytes=VMEM_LIMIT_BYTES)


def _tile(n, preferred):
    return preferred if n % preferred == 0 else n


def _dot(a, b):
    return jnp.dot(a, b, preferred_element_type=F32)


def _dot_nt(a, b):
    return lax.dot_general(a, b, (((1,), (1,)), ((), ())),
                           preferred_element_type=F32)


def _rmsnorm_kernel(x_ref, g_ref, o_ref):
    x = x_ref[...]
    ms = jnp.mean(x * x, axis=-1, keepdims=True)
    o_ref[...] = (x * lax.rsqrt(ms + RMS_EPS) * g_ref[...]).astype(o_ref.dtype)


def _rmsnorm(x, g):
    s, d = x.shape
    tm = _tile(s, 256)
    return pl.pallas_call(
        _rmsnorm_kernel,
        grid=(s // tm,),
        in_specs=[pl.BlockSpec((tm, d), lambda i: (i, 0)),
                  pl.BlockSpec((1, d), lambda i: (0, 0))],
        out_specs=pl.BlockSpec((tm, d), lambda i: (i, 0)),
        out_shape=jax.ShapeDtypeStruct((s, d), BF16),
        compiler_params=_params("arbitrary"),
        name="rmsnorm",
    )(x, g.reshape(1, d))


def _rope_table_kernel(pos_ref, invf_ref, sign_ref, cos_ref, sin_ref):
    ang = pos_ref[...].astype(F32) * invf_ref[...]
    cos_ref[...] = jnp.cos(ang)
    sin_ref[...] = jnp.sin(ang) * sign_ref[...]


def _rope_tables(positions):
    s = positions.shape[0]
    half = HEAD_DIM // 2
    inv_freq = ROPE_THETA ** (-jnp.arange(half, dtype=F32) / half)
    invf = jnp.concatenate([inv_freq, inv_freq]).reshape(1, HEAD_DIM)
    sign = jnp.concatenate([-jnp.ones((half,), F32),
                            jnp.ones((half,), F32)]).reshape(1, HEAD_DIM)
    tm = _tile(s, 1024)
    row = pl.BlockSpec((1, HEAD_DIM), lambda i: (0, 0))
    tab = pl.BlockSpec((tm, HEAD_DIM), lambda i: (i, 0))
    return pl.pallas_call(
        _rope_table_kernel,
        grid=(s // tm,),
        in_specs=[pl.BlockSpec((tm, 1), lambda i: (i, 0)), row, row],
        out_specs=[tab, tab],
        out_shape=[jax.ShapeDtypeStruct((s, HEAD_DIM), F32)] * 2,
        compiler_params=_params("arbitrary"),
        name="rope_tables",
    )(positions.reshape(s, 1), invf, sign)


def _qk_proj_kernel(h_ref, w_ref, g_ref, cos_ref, sin_ref, o_ref):
    acc = _dot(h_ref[...], w_ref[...])
    cos = cos_ref[...]
    sin = sin_ref[...]
    for hd in range(acc.shape[1] // HEAD_DIM):
        cols = slice(hd * HEAD_DIM, (hd + 1) * HEAD_DIM)
        a = acc[:, cols]
        ms = jnp.mean(a * a, axis=-1, keepdims=True)
        y = a * lax.rsqrt(ms + RMS_EPS) * g_ref[:, cols]
        r = y * cos + pltpu.roll(y, HEAD_DIM // 2, axis=1) * sin
        o_ref[:, cols] = r.astype(o_ref.dtype)


def _plain_proj_kernel(h_ref, w_ref, o_ref, *, sigmoid):
    acc = _dot(h_ref[...], w_ref[...])
    if sigmoid:
        acc = jax.nn.sigmoid(acc)
    o_ref[...] = acc.astype(o_ref.dtype)


def _qk_proj(h, w_in, gains, cos, sin):
    s, d = h.shape
    n = 2 * D_ATTN
    tm, tn = _tile(s, 1024), 1024
    tab = pl.BlockSpec((tm, HEAD_DIM), lambda j, i: (i, 0))
    return pl.pallas_call(
        _qk_proj_kernel,
        grid=(n // tn, s // tm),
        in_specs=[pl.BlockSpec((tm, d), lambda j, i: (i, 0)),
                  pl.BlockSpec((d, tn), lambda j, i: (0, j)),
                  pl.BlockSpec((1, tn), lambda j, i: (0, j)),
                  tab, tab],
        out_specs=pl.BlockSpec((tm, tn), lambda j, i: (i, j)),
        out_shape=jax.ShapeDtypeStruct((s, n), BF16),
        compiler_params=_params("arbitrary", "arbitrary"),
        name="qk_proj",
    )(h, w_in, gains, cos, sin)


def _plain_proj(h, w_in, col0, n, *, sigmoid):
    s, d = h.shape
    tm, tn = _tile(s, 1024), 1024
    j0 = col0 // tn
    return pl.pallas_call(
        functools.partial(_plain_proj_kernel, sigmoid=sigmoid),
        grid=(n // tn, s // tm),
        in_specs=[pl.BlockSpec((tm, d), lambda j, i: (i, 0)),
                  pl.BlockSpec((d, tn), lambda j, i: (0, j0 + j))],
        out_specs=pl.BlockSpec((tm, tn), lambda j, i: (i, j)),
        out_shape=jax.ShapeDtypeStruct((s, n), BF16),
        compiler_params=_params("arbitrary", "arbitrary"),
        name="gate_proj" if sigmoid else "v_proj",
    )(h, w_in)


def _causal_conv3(m, w, carry_ref):
    tm = m.shape[0]
    rows = lax.broadcasted_iota(jnp.int32, m.shape, 0)
    prev = carry_ref[...]
    c2 = prev[6:7]
    c1 = prev[7:8]
    m1 = jnp.where(rows == 0, c1, pltpu.roll(m, 1, axis=0))
    m2 = jnp.where(rows == 0, c2,
                   jnp.where(rows == 1, c1, pltpu.roll(m, 2, axis=0)))
    carry_ref[...] = m[tm - 8:tm]
    return w[2:3] * m + w[0:1] * m2 + w[1:2] * m1


def _conv_proj_kernel(h_ref, wb_ref, wc_ref, wx_ref, cw_ref, o_ref, carry_ref):
    @pl.when(pl.program_id(1) == 0)
    def _():
        carry_ref[...] = jnp.zeros_like(carry_ref)

    h = h_ref[...]
    cb = _dot(h, wb_ref[...])
    m = _dot(h, wc_ref[...]) * _dot(h, wx_ref[...])
    y = _causal_conv3(m, cw_ref[...], carry_ref)
    o_ref[...] = (cb * y).astype(o_ref.dtype)


def _conv_proj(h, w_in, conv_w, col0):
    s, d = h.shape
    tm, tn = _tile(s, 1024), 256
    jb, jc, jx = (col0 // tn, (col0 + D_CONV) // tn, (col0 + 2 * D_CONV) // tn)
    return pl.pallas_call(
        _conv_proj_kernel,
        grid=(D_CONV // tn, s // tm),
        in_specs=[pl.BlockSpec((tm, d), lambda j, i: (i, 0)),
                  pl.BlockSpec((d, tn), lambda j, i: (0, jb + j)),
                  pl.BlockSpec((d, tn), lambda j, i: (0, jc + j)),
                  pl.BlockSpec((d, tn), lambda j, i: (0, jx + j)),
                  pl.BlockSpec((3, tn), lambda j, i: (0, j))],
        out_specs=pl.BlockSpec((tm, tn), lambda j, i: (i, j)),
        out_shape=jax.ShapeDtypeStruct((s, D_CONV), BF16),
        scratch_shapes=[pltpu.VMEM((8, tn), F32)],
        compiler_params=_params("arbitrary", "arbitrary"),
        name="conv_proj",
    )(h, w_in, w_in, w_in, conv_w)


def _attn_kernel(q_ref, k_ref, v_ref, o_ref,
                 kaug_ref, qaug_ref, kmean_ref, m_ref, l_ref, acc_ref,
                 *, n_blocks, group):
    hd = pl.program_id(0)
    qb = pl.program_id(1)
    blk, d = MOBA_BLOCK, HEAD_DIM
    lane = lax.broadcasted_iota(jnp.int32, (blk, d), 1)

    @pl.when((hd == 0) & (qb == 0))
    def _():
        kmean_ref[...] = jnp.zeros_like(kmean_ref)

        def body(b, carry):
            rows = pl.ds(pl.multiple_of(b * blk, blk), blk)
            kaug_ref[rows, d:2 * d] = (lane == b).astype(BF16)
            return carry
        lax.fori_loop(0, n_blocks, body, 0)

    @pl.when(qb == 0)
    def _():
        def body(b, carry):
            rows = pl.ds(pl.multiple_of(b * blk, blk), blk)
            kb = k_ref[rows, :]
            kaug_ref[rows, 0:d] = kb
            kmean_ref[pl.ds(b, 1), :] = (
                jnp.sum(kb.astype(F32), axis=0, keepdims=True) * (1.0 / blk))
            return carry
        lax.fori_loop(0, n_blocks, body, 0)

    q = q_ref[...]

    gate = _dot_nt(q, kmean_ref[...].astype(BF16))
    gate = jnp.where(lane < qb, gate, -jnp.inf)
    sel = jnp.zeros((blk, d), jnp.bool_)
    for _ in range(MOBA_TOPK):
        mx = jnp.max(gate, axis=1, keepdims=True)
        idx = jnp.min(jnp.where(gate == mx, lane, d), axis=1, keepdims=True)
        hit = lane == idx
        sel = sel | (hit & (mx > -jnp.inf))
        gate = jnp.where(hit, -jnp.inf, gate)
    qaug_ref[:, 0:d] = q
    qaug_ref[:, d:2 * d] = jnp.where(sel, 0.0, MASK_VALUE).astype(BF16)

    own = pl.ds(pl.multiple_of(qb * blk, blk), blk)
    s = _dot_nt(q, k_ref[own, :])
    row = lax.broadcasted_iota(jnp.int32, (blk, blk), 0)
    col = lax.broadcasted_iota(jnp.int32, (blk, blk), 1)
    s = jnp.where(col <= row, s, MASK_VALUE)
    m0 = jnp.max(s, axis=1, keepdims=True)
    p = jnp.exp(s - m0)
    m_ref[...] = jnp.broadcast_to(m0, (blk, LANES))
    l_ref[...] = jnp.broadcast_to(jnp.sum(p, axis=1, keepdims=True), (blk, LANES))
    acc_ref[...] = _dot(p.astype(BF16), v_ref[own, :])

    width = group * blk

    def body(gi, carry):
        keys = pl.ds(pl.multiple_of(gi * width, width), width)
        s = _dot_nt(qaug_ref[...], kaug_ref[keys, :])
        m_prev = m_ref[...]
        m_next = jnp.maximum(m_prev, jnp.max(s, axis=1, keepdims=True))
        p = jnp.exp(s - jnp.tile(m_next, (1, width // LANES)))
        alpha = jnp.exp(m_prev - m_next)
        l_ref[...] = alpha * l_ref[...] + jnp.sum(p, axis=1, keepdims=True)
        acc_ref[...] = acc_ref[...] * alpha + _dot(p.astype(BF16), v_ref[keys, :])
        m_ref[...] = m_next
        return carry
    lax.fori_loop(0, (qb + group - 1) // group, body, 0)

    o_ref[...] = (acc_ref[...] / l_ref[...]).astype(o_ref.dtype)


def _moba_attention(qk, v):
    s = v.shape[0]
    assert s % MOBA_BLOCK == 0 and s // MOBA_BLOCK <= LANES
    n_blocks = s // MOBA_BLOCK
    group = 2 if n_blocks % 2 == 0 else 1
    blk, d = MOBA_BLOCK, HEAD_DIM
    return pl.pallas_call(
        functools.partial(_attn_kernel, n_blocks=n_blocks, group=group),
        grid=(N_ATTN_HEADS, n_blocks),
        in_specs=[pl.BlockSpec((blk, d), lambda h, i: (i, h)),
                  pl.BlockSpec((s, d), lambda h, i: (0, N_ATTN_HEADS + h)),
                  pl.BlockSpec((s, d), lambda h, i: (0, h))],
        out_specs=pl.BlockSpec((blk, d), lambda h, i: (i, h)),
        out_shape=jax.ShapeDtypeStruct((s, D_ATTN), BF16),
        scratch_shapes=[pltpu.VMEM((s, 2 * d), BF16),
                        pltpu.VMEM((blk, 2 * d), BF16),
                        pltpu.VMEM((LANES, d), F32),
                        pltpu.VMEM((blk, LANES), F32),
                        pltpu.VMEM((blk, LANES), F32),
                        pltpu.VMEM((blk, d), F32)],
        compiler_params=_params("arbitrary", "arbitrary"),
        name="moba_attention",
    )(qk, qk, v)


def _merge_kernel(a_ref, c_ref, wa_ref, wc_ref, ga_ref, gc_ref, o_ref):
    ya = _dot(a_ref[...], wa_ref[...])
    yc = _dot(c_ref[...], wc_ref[...])
    o_ref[...] = (ga_ref[...].astype(F32) * ya
                  + gc_ref[...].astype(F32) * yc).astype(o_ref.dtype)


def _merge(attn, conv, w_a, w_c, gates):
    s = attn.shape[0]
    n = w_a.shape[1]
    tm, tn = _tile(s, 1024), 1024
    jc = n // tn
    return pl.pallas_call(
        _merge_kernel,
        grid=(n // tn, s // tm),
        in_specs=[pl.BlockSpec((tm, D_ATTN), lambda j, i: (i, 0)),
                  pl.BlockSpec((tm, D_CONV), lambda j, i: (i, 0)),
                  pl.BlockSpec((D_ATTN, tn), lambda j, i: (0, j)),
                  pl.BlockSpec((D_CONV, tn), lambda j, i: (0, j)),
                  pl.BlockSpec((tm, tn), lambda j, i: (i, j)),
                  pl.BlockSpec((tm, tn), lambda j, i: (i, jc + j))],
        out_specs=pl.BlockSpec((tm, tn), lambda j, i: (i, j)),
        out_shape=jax.ShapeDtypeStruct((s, n), BF16),
        compiler_params=_params("arbitrary", "arbitrary"),
        name="merge",
    )(attn, conv, w_a, w_c, gates, gates)


def _residual_proj_kernel(a_ref, w_ref, x_ref, o_ref):
    o_ref[...] = x_ref[...] + _dot(a_ref[...], w_ref[...])


def _residual_proj(a, w, x):
    s, k = a.shape
    n = w.shape[1]
    tm, tn = _tile(s, 1024), 1024
    return pl.pallas_call(
        _residual_proj_kernel,
        grid=(n // tn, s // tm),
        in_specs=[pl.BlockSpec((tm, k), lambda j, i: (i, 0)),
                  pl.BlockSpec((k, tn), lambda j, i: (0, j)),
                  pl.BlockSpec((tm, tn), lambda j, i: (i, j))],
        out_specs=pl.BlockSpec((tm, tn), lambda j, i: (i, j)),
        out_shape=jax.ShapeDtypeStruct((s, n), F32),
        compiler_params=_params("arbitrary", "arbitrary"),
        name="out_proj",
    )(a, w, x)


def _ffn_up_kernel(h_ref, wg_ref, wv_ref, cwg_ref, cwv_ref, o_ref, cg_ref, cv_ref):
    j = pl.program_id(1)

    @pl.when(pl.program_id(0) == 0)
    def _():
        cg_ref[j] = jnp.zeros(cg_ref.shape[1:], F32)
        cv_ref[j] = jnp.zeros(cv_ref.shape[1:], F32)

    h = h_ref[...]
    ug = _causal_conv3(_dot(h, wg_ref[...]), cwg_ref[...], cg_ref.at[j])
    uv = _causal_conv3(_dot(h, wv_ref[...]), cwv_ref[...], cv_ref.at[j])
    o_ref[...] = (ug * jax.nn.sigmoid(ug) * uv).astype(o_ref.dtype)


def _ffn_up(h, wg, wv, cwg, cwv):
    s, d = h.shape
    f = wg.shape[1]
    tm, tn = _tile(s, 1024), 512
    nj = f // tn
    return pl.pallas_call(
        _ffn_up_kernel,
        grid=(s // tm, nj),
        in_specs=[pl.BlockSpec((tm, d), lambda i, j: (i, 0)),
                  pl.BlockSpec((d, tn), lambda i, j: (0, j)),
                  pl.BlockSpec((d, tn), lambda i, j: (0, j)),
                  pl.BlockSpec((3, tn), lambda i, j: (0, j)),
                  pl.BlockSpec((3, tn), lambda i, j: (0, j))],
        out_specs=pl.BlockSpec((tm, tn), lambda i, j: (i, j)),
        out_shape=jax.ShapeDtypeStruct((s, f), BF16),
        scratch_shapes=[pltpu.VMEM((nj, 8, tn), F32),
                        pltpu.VMEM((nj, 8, tn), F32)],
        compiler_params=_params("arbitrary", "arbitrary"),
        name="ffn_up",
    )(h, wg, wv, cwg, cwv)


def _ffn_down_kernel(a_ref, w_ref, x_ref, o_ref, ob_ref, acc_ref):
    k = pl.program_id(2)

    @pl.when(k == 0)
    def _():
        acc_ref[...] = x_ref[...]

    acc_ref[...] += _dot(a_ref[...], w_ref[...])

    @pl.when(k == pl.num_programs(2) - 1)
    def _():
        o_ref[...] = acc_ref[...]
        ob_ref[...] = acc_ref[...].astype(ob_ref.dtype)


def _ffn_down(act, w, x):
    s, f = act.shape
    n = w.shape[1]
    tm, tn = _tile(s, 1024), 1024
    tk = _tile(f, 2816)
    out = pl.BlockSpec((tm, tn), lambda i, j, k: (i, j))
    return pl.pallas_call(
        _ffn_down_kernel,
        grid=(s // tm, n // tn, f // tk),
        in_specs=[pl.BlockSpec((tm, tk), lambda i, j, k: (i, k)),
                  pl.BlockSpec((tk, tn), lambda i, j, k: (k, j)),
                  out],
        out_specs=[out, out],
        out_shape=[jax.ShapeDtypeStruct((s, n), F32),
                   jax.ShapeDtypeStruct((s, n), BF16)],
        scratch_shapes=[pltpu.VMEM((tm, tn), F32)],
        compiler_params=_params("arbitrary", "arbitrary", "arbitrary"),
        name="ffn_down",
    )(act, w, x)


def _ple_kernel(xb_ref, wpg_ref, p_ref, wp_ref, x_ref, o_ref):
    gate = jax.nn.sigmoid(_dot(xb_ref[...], wpg_ref[...]))
    emb = _dot(p_ref[...].astype(BF16), wp_ref[...])
    o_ref[...] = x_ref[...] + gate * emb


def _ple(xb, w_pg, p, w_p, x):
    s, d = xb.shape
    n = w_pg.shape[1]
    dp = p.shape[1]
    tm, tn = _tile(s, 1024), 512
    return pl.pallas_call(
        _ple_kernel,
        grid=(n // tn, s // tm),
        in_specs=[pl.BlockSpec((tm, d), lambda j, i: (i, 0)),
                  pl.BlockSpec((d, tn), lambda j, i: (0, j)),
                  pl.BlockSpec((tm, dp), lambda j, i: (i, 0)),
                  pl.BlockSpec((dp, tn), lambda j, i: (0, j)),
                  pl.BlockSpec((tm, tn), lambda j, i: (i, j))],
        out_specs=pl.BlockSpec((tm, tn), lambda j, i: (i, j)),
        out_shape=jax.ShapeDtypeStruct((s, n), F32),
        compiler_params=_params("arbitrary", "arbitrary"),
        name="ple",
    )(xb, w_pg, p, w_p, x)


def _pad_cols(a, n):
    return jnp.pad(a, ((0, 0), (0, n - a.shape[1])))


def _layer(x, p, cos, sin, w_norm_mix, w_in, q_norm, k_norm, conv_mix_w,
           w_attn_branch, w_conv_branch, w_out, w_norm_ffn, w_up, ffn_conv_w,
           w_down, w_ple, w_ple_gate):
    d_ff = w_down.shape[0]
    d_ff_pad = -(-d_ff // 2816) * 2816 if d_ff > 2816 else d_ff
    w_in_b = w_in.astype(BF16)

    gains = jnp.concatenate([jnp.tile(q_norm * (HEAD_DIM ** -0.5), N_ATTN_HEADS),
                             jnp.tile(k_norm, N_ATTN_HEADS)]).reshape(1, 2 * D_ATTN)

    h = _rmsnorm(x, w_norm_mix)
    qk = _qk_proj(h, w_in_b, gains, cos, sin)
    v = _plain_proj(h, w_in_b, 2 * D_ATTN, D_ATTN, sigmoid=False)
    conv = _conv_proj(h, w_in_b, conv_mix_w, 3 * D_ATTN)
    gates = _plain_proj(h, w_in_b, 3 * D_ATTN + 3 * D_CONV, 2 * x.shape[1],
                        sigmoid=True)
    attn = _moba_attention(qk, v)
    merged = _merge(attn, conv, w_attn_branch.astype(BF16),
                    w_conv_branch.astype(BF16), gates)
    x1 = _residual_proj(merged, w_out.astype(BF16), x)

    h2 = _rmsnorm(x1, w_norm_ffn)
    wg = _pad_cols(w_up[:, :d_ff].astype(BF16), d_ff_pad)
    wv = _pad_cols(w_up[:, d_ff:].astype(BF16), d_ff_pad)
    cwg = _pad_cols(ffn_conv_w[:, :d_ff], d_ff_pad)
    cwv = _pad_cols(ffn_conv_w[:, d_ff:], d_ff_pad)
    act = _ffn_up(h2, wg, wv, cwg, cwv)
    w_down_b = jnp.pad(w_down.astype(BF16), ((0, d_ff_pad - d_ff), (0, 0)))
    x2, x2b = _ffn_down(act, w_down_b, x1)

    return _ple(x2b, w_ple_gate.astype(BF16), p, w_ple.astype(BF16), x2)


def kernel(x, p, positions, w_norm_mix, w_in, q_norm, k_norm, conv_mix_w,
           w_attn_branch, w_conv_branch, w_out, w_norm_ffn, w_up, ffn_conv_w,
           w_down, w_ple, w_ple_gate):
    b, s, d = x.shape
    assert b == 1, "single-sequence prefill only"
    xs = x.reshape(s, d)
    cos, sin = _rope_tables(positions.reshape(s))
    for i in range(w_in.shape[0]):
        xs = _layer(xs, p[i].reshape(s, -1), cos, sin, w_norm_mix[i], w_in[i],
                    q_norm[i], k_norm[i], conv_mix_w[i], w_attn_branch[i],
                    w_conv_branch[i], w_out[i], w_norm_ffn[i], w_up[i],
                    ffn_conv_w[i], w_down[i], w_ple[i], w_ple_gate[i])
    return xs.reshape(b, s, d)
```

```python
import functools
import math

import jax
import jax.numpy as jnp
from jax import lax
from jax.experimental import pallas as pl
from jax.experimental.pallas import tpu as pltpu

N_ATTN_HEADS = 16
HEAD_DIM = 128
MOBA_BLOCK = 256
MOBA_TOPK = 3
ROPE_THETA = 10000.0
RMS_EPS = 1e-6
D_ATTN = N_ATTN_HEADS * HEAD_DIM
D_CONV = 2048

LANES = 128
MASK_VALUE = -1e30
LOG2_E = math.log2(math.e)
VMEM_LIMIT_BYTES = 56 * 1024 * 1024
BF16 = jnp.bfloat16
F32 = jnp.float32


def _params(*semantics):
    return pltpu.CompilerParams(dimension_semantics=semantics,
                                vmem_limit_bytes=VMEM_LIMIT_BYTES)


def _tile(n, preferred):
    return preferred if n % preferred == 0 else n


def _dot(a, b):
    return jnp.dot(a, b, preferred_element_type=F32)


def _dot_nt(a, b):
    return lax.dot_general(a, b, (((1,), (1,)), ((), ())),
                           preferred_element_type=F32)


def _rmsnorm_kernel(x_ref, g_ref, o_ref):
    x = x_ref[...]
    ms = jnp.mean(x * x, axis=-1, keepdims=True)
    o_ref[...] = (x * lax.rsqrt(ms + RMS_EPS) * g_ref[...]).astype(o_ref.dtype)


def _rmsnorm(x, g):
    s, d = x.shape
    tm = _tile(s, 256)
    return pl.pallas_call(
        _rmsnorm_kernel,
        grid=(s // tm,),
        in_specs=[pl.BlockSpec((tm, d), lambda i: (i, 0)),
                  pl.BlockSpec((1, d), lambda i: (0, 0))],
        out_specs=pl.BlockSpec((tm, d), lambda i: (i, 0)),
        out_shape=jax.ShapeDtypeStruct((s, d), BF16),
        compiler_params=_params("arbitrary"),
        name="rmsnorm",
    )(x, g.reshape(1, d))


def _rope_table_kernel(pos_ref, invf_ref, sign_ref, cos_ref, sin_ref):
    ang = pos_ref[...].astype(F32) * invf_ref[...]
    cos_ref[...] = jnp.cos(ang)
    sin_ref[...] = jnp.sin(ang) * sign_ref[...]


def _rope_tables(positions):
    s = positions.shape[0]
    half = HEAD_DIM // 2
    inv_freq = ROPE_THETA ** (-jnp.arange(half, dtype=F32) / half)
    invf = jnp.concatenate([inv_freq, inv_freq]).reshape(1, HEAD_DIM)
    sign = jnp.concatenate([-jnp.ones((half,), F32),
                            jnp.ones((half,), F32)]).reshape(1, HEAD_DIM)
    tm = _tile(s, 1024)
    row = pl.BlockSpec((1, HEAD_DIM), lambda i: (0, 0))
    tab = pl.BlockSpec((tm, HEAD_DIM), lambda i: (i, 0))
    return pl.pallas_call(
        _rope_table_kernel,
        grid=(s // tm,),
        in_specs=[pl.BlockSpec((tm, 1), lambda i: (i, 0)), row, row],
        out_specs=[tab, tab],
        out_shape=[jax.ShapeDtypeStruct((s, HEAD_DIM), F32)] * 2,
        compiler_params=_params("arbitrary"),
        name="rope_tables",
    )(positions.reshape(s, 1), invf, sign)


def _qk_proj_kernel(h_ref, w_ref, g_ref, cos_ref, sin_ref, o_ref):
    acc = _dot(h_ref[...], w_ref[...])
    cos = cos_ref[...]
    sin = sin_ref[...]
    for hd in range(acc.shape[1] // HEAD_DIM):
        cols = slice(hd * HEAD_DIM, (hd + 1) * HEAD_DIM)
        a = acc[:, cols]
        ms = jnp.mean(a * a, axis=-1, keepdims=True)
        y = a * lax.rsqrt(ms + RMS_EPS) * g_ref[:, cols]
        r = y * cos + pltpu.roll(y, HEAD_DIM // 2, axis=1) * sin
        o_ref[:, cols] = r.astype(o_ref.dtype)


def _plain_proj_kernel(h_ref, w_ref, o_ref, *, sigmoid):
    acc = _dot(h_ref[...], w_ref[...])
    if sigmoid:
        acc = jax.nn.sigmoid(acc)
    o_ref[...] = acc.astype(o_ref.dtype)


def _qk_proj(h, w_in, gains, cos, sin):
    s, d = h.shape
    n = 2 * D_ATTN
    tm, tn = _tile(s, 1024), 1024
    tab = pl.BlockSpec((tm, HEAD_DIM), lambda j, i: (i, 0))
    return pl.pallas_call(
        _qk_proj_kernel,
        grid=(n // tn, s // tm),
        in_specs=[pl.BlockSpec((tm, d), lambda j, i: (i, 0)),
                  pl.BlockSpec((d, tn), lambda j, i: (0, j)),
                  pl.BlockSpec((1, tn), lambda j, i: (0, j)),
                  tab, tab],
        out_specs=pl.BlockSpec((tm, tn), lambda j, i: (i, j)),
        out_shape=jax.ShapeDtypeStruct((s, n), BF16),
        compiler_params=_params("arbitrary", "arbitrary"),
        name="qk_proj",
    )(h, w_in, gains, cos, sin)


def _plain_proj(h, w_in, col0, n, *, sigmoid):
    s, d = h.shape
    tm, tn = _tile(s, 1024), 1024
    j0 = col0 // tn
    return pl.pallas_call(
        functools.partial(_plain_proj_kernel, sigmoid=sigmoid),
        grid=(n // tn, s // tm),
        in_specs=[pl.BlockSpec((tm, d), lambda j, i: (i, 0)),
                  pl.BlockSpec((d, tn), lambda j, i: (0, j0 + j))],
        out_specs=pl.BlockSpec((tm, tn), lambda j, i: (i, j)),
        out_shape=jax.ShapeDtypeStruct((s, n), BF16),
        compiler_params=_params("arbitrary", "arbitrary"),
        name="gate_proj" if sigmoid else "v_proj",
    )(h, w_in)


def _causal_conv3(m, w, carry_ref):
    tm = m.shape[0]
    rows = lax.broadcasted_iota(jnp.int32, m.shape, 0)
    prev = carry_ref[...]
    c2 = prev[6:7]
    c1 = prev[7:8]
    m1 = jnp.where(rows == 0, c1, pltpu.roll(m, 1, axis=0))
    m2 = jnp.where(rows == 0, c2,
                   jnp.where(rows == 1, c1, pltpu.roll(m, 2, axis=0)))
    carry_ref[...] = m[tm - 8:tm]
    return w[2:3] * m + w[0:1] * m2 + w[1:2] * m1


def _conv_proj_kernel(h_ref, wb_ref, wc_ref, wx_ref, cw_ref, o_ref, carry_ref):
    @pl.when(pl.program_id(1) == 0)
    def _():
        carry_ref[...] = jnp.zeros_like(carry_ref)

    h = h_ref[...]
    cb = _dot(h, wb_ref[...])
    m = _dot(h, wc_ref[...]) * _dot(h, wx_ref[...])
    y = _causal_conv3(m, cw_ref[...], carry_ref)
    o_ref[...] = (cb * y).astype(o_ref.dtype)


def _conv_proj(h, w_in, conv_w, col0):
    s, d = h.shape
    tm, tn = _tile(s, 1024), 256
    jb, jc, jx = (col0 // tn, (col0 + D_CONV) // tn, (col0 + 2 * D_CONV) // tn)
    return pl.pallas_call(
        _conv_proj_kernel,
        grid=(D_CONV // tn, s // tm),
        in_specs=[pl.BlockSpec((tm, d), lambda j, i: (i, 0)),
                  pl.BlockSpec((d, tn), lambda j, i: (0, jb + j)),
                  pl.BlockSpec((d, tn), lambda j, i: (0, jc + j)),
                  pl.BlockSpec((d, tn), lambda j, i: (0, jx + j)),
                  pl.BlockSpec((3, tn), lambda j, i: (0, j))],
        out_specs=pl.BlockSpec((tm, tn), lambda j, i: (i, j)),
        out_shape=jax.ShapeDtypeStruct((s, D_CONV), BF16),
        scratch_shapes=[pltpu.VMEM((8, tn), F32)],
        compiler_params=_params("arbitrary", "arbitrary"),
        name="conv_proj",
    )(h, w_in, w_in, w_in, conv_w)


TILE_BLOCKS = 2
ATTN_TILE = TILE_BLOCKS * MOBA_BLOCK
ONES_ROWS = 16
NEVER_BLOCK = LANES - 1


def _attn_kernel(qt_ref, k_ref, vt_ref, o_ref,
                 kaug_ref, qaug_ref, kmean_ref, sa_ref, sb_ref, acc_ref,
                 *, n_blocks):
    hd = pl.program_id(0)
    qi = pl.program_id(1)
    blk, d, tile = MOBA_BLOCK, HEAD_DIM, ATTN_TILE
    dummy_tile = n_blocks // TILE_BLOCKS

    @pl.when((hd == 0) & (qi == 0))
    def _():
        kmean_ref[...] = jnp.zeros_like(kmean_ref)
        lane = lax.broadcasted_iota(jnp.int32, (blk, d), 1)

        def body(b, carry):
            rows = pl.ds(pl.multiple_of(b * blk, blk), blk)
            kaug_ref[rows, d:2 * d] = (lane == b).astype(BF16)
            return carry
        lax.fori_loop(0, n_blocks, body, 0)
        for u in range(TILE_BLOCKS):
            rows = pl.ds((n_blocks + u) * blk, blk)
            kaug_ref[rows, 0:d] = jnp.zeros((blk, d), BF16)
            kaug_ref[rows, d:2 * d] = (lane == NEVER_BLOCK).astype(BF16)

    @pl.when(qi == 0)
    def _():
        def body(b, carry):
            rows = pl.ds(pl.multiple_of(b * blk, blk), blk)
            kb = k_ref[rows, :]
            kaug_ref[rows, 0:d] = kb
            kmean_ref[pl.ds(b, 1), :] = (
                jnp.sum(kb.astype(F32), axis=0, keepdims=True) * (1.0 / blk))
            return carry
        lax.fori_loop(0, n_blocks, body, 0)

    qt = qt_ref[...]

    blk_id = lax.broadcasted_iota(jnp.int32, (LANES, tile), 0)
    q_lane = lax.broadcasted_iota(jnp.int32, (LANES, tile), 1)
    own_blk = qi * TILE_BLOCKS + q_lane // blk
    gate = _dot(kmean_ref[...].astype(BF16), qt)
    gate = jnp.where(blk_id < own_blk, gate, -jnp.inf)
    sel = blk_id == own_blk
    for _ in range(MOBA_TOPK):
        mx = jnp.max(gate, axis=0, keepdims=True)
        idx = jnp.min(jnp.where(gate == mx, blk_id, LANES), axis=0, keepdims=True)
        hit = blk_id == idx
        sel = sel | (hit & (mx > -jnp.inf))
        gate = jnp.where(hit, -jnp.inf, gate)
    qaug_ref[0:d, :] = qt
    qaug_ref[d:2 * d, :] = jnp.where(sel, 0.0, MASK_VALUE).astype(BF16)

    def scores(t):
        keys = pl.ds(pl.multiple_of(t * tile, tile), tile)
        return _dot(kaug_ref[keys, :], qaug_ref[...])

    def weighted_values(t, pb):
        out = _dot(vt_ref[0, t * TILE_BLOCKS], pb[0:blk])
        for u in range(1, TILE_BLOCKS):
            out += _dot(vt_ref[0, t * TILE_BLOCKS + u], pb[u * blk:(u + 1) * blk])
        return out

    def past_tile(j):
        return jnp.where(j < qi, j, dummy_tile)

    sa_ref[...] = scores(qi)
    sb_ref[...] = scores(past_tile(0))

    key_pos = lax.broadcasted_iota(jnp.int32, (tile, tile), 0)
    q_pos = lax.broadcasted_iota(jnp.int32, (tile, tile), 1)
    s = jnp.where(key_pos <= q_pos, sa_ref[...], MASK_VALUE)
    m0 = jnp.max(s, axis=0, keepdims=True)
    acc_ref[...] = weighted_values(qi, jnp.exp2(s - m0).astype(BF16))

    def update(s_ref, t, m_prev):
        s = s_ref[...]
        m_next = jnp.maximum(m_prev, jnp.max(s, axis=0, keepdims=True))
        pb = jnp.exp2(s - m_next).astype(BF16)
        alpha = jnp.exp2(m_prev - m_next)
        acc_ref[...] = acc_ref[...] * alpha + weighted_values(t, pb)
        return m_next

    def body(i, m):
        j = 2 * i
        sa_ref[...] = scores(past_tile(j + 1))
        m = update(sb_ref, j, m)
        sb_ref[...] = scores(past_tile(j + 2))
        return update(sa_ref, past_tile(j + 1), m)

    lax.fori_loop(0, lax.shift_right_logical(qi + 1, 1), body, m0)
    acc = acc_ref[...]
    o_ref[...] = (acc[0:d] / acc[d:d + 1]).T.astype(o_ref.dtype)


def _moba_attention(qk, v):
    s = v.shape[0]
    n_blocks = s // MOBA_BLOCK
    assert s % ATTN_TILE == 0 and n_blocks <= NEVER_BLOCK
    blk, d, tile = MOBA_BLOCK, HEAD_DIM, ATTN_TILE
    qt = qk[:, :D_ATTN].T
    vt = v.reshape(n_blocks, blk, N_ATTN_HEADS, d).transpose(2, 0, 3, 1)
    vt = jnp.concatenate(
        [vt, jnp.ones((N_ATTN_HEADS, n_blocks, ONES_ROWS, blk), BF16)], axis=2)
    vt = jnp.pad(vt, ((0, 0), (0, TILE_BLOCKS), (0, 0), (0, 0)))
    return pl.pallas_call(
        functools.partial(_attn_kernel, n_blocks=n_blocks),
        grid=(N_ATTN_HEADS, s // tile),
        in_specs=[pl.BlockSpec((d, tile), lambda h, i: (h, i)),
                  pl.BlockSpec((s, d), lambda h, i: (0, N_ATTN_HEADS + h)),
                  pl.BlockSpec((1, n_blocks + TILE_BLOCKS, d + ONES_ROWS, blk),
                               lambda h, i: (h, 0, 0, 0))],
        out_specs=pl.BlockSpec((tile, d), lambda h, i: (i, h)),
        out_shape=jax.ShapeDtypeStruct((s, D_ATTN), BF16),
        scratch_shapes=[pltpu.VMEM((s + tile, 2 * d), BF16),
                        pltpu.VMEM((2 * d, tile), BF16),
                        pltpu.VMEM((LANES, d), F32),
                        pltpu.VMEM((tile, tile), F32),
                        pltpu.VMEM((tile, tile), F32),
                        pltpu.VMEM((d + ONES_ROWS, tile), F32)],
        compiler_params=_params("arbitrary", "arbitrary"),
        name="moba_attention",
    )(qt, qk, vt)


def _merge_kernel(a_ref, c_ref, wa_ref, wc_ref, ga_ref, gc_ref, o_ref):
    ya = _dot(a_ref[...], wa_ref[...])
    yc = _dot(c_ref[...], wc_ref[...])
    o_ref[...] = (ga_ref[...].astype(F32) * ya
                  + gc_ref[...].astype(F32) * yc).astype(o_ref.dtype)


def _merge(attn, conv, w_a, w_c, gates):
    s = attn.shape[0]
    n = w_a.shape[1]
    tm, tn = _tile(s, 1024), 1024
    jc = n // tn
    return pl.pallas_call(
        _merge_kernel,
        grid=(n // tn, s // tm),
        in_specs=[pl.BlockSpec((tm, D_ATTN), lambda j, i: (i, 0)),
                  pl.BlockSpec((tm, D_CONV), lambda j, i: (i, 0)),
                  pl.BlockSpec((D_ATTN, tn), lambda j, i: (0, j)),
                  pl.BlockSpec((D_CONV, tn), lambda j, i: (0, j)),
                  pl.BlockSpec((tm, tn), lambda j, i: (i, j)),
                  pl.BlockSpec((tm, tn), lambda j, i: (i, jc + j))],
        out_specs=pl.BlockSpec((tm, tn), lambda j, i: (i, j)),
        out_shape=jax.ShapeDtypeStruct((s, n), BF16),
        compiler_params=_params("arbitrary", "arbitrary"),
        name="merge",
    )(attn, conv, w_a, w_c, gates, gates)


def _residual_proj_kernel(a_ref, w_ref, x_ref, o_ref):
    o_ref[...] = x_ref[...] + _dot(a_ref[...], w_ref[...])


def _residual_proj(a, w, x):
    s, k = a.shape
    n = w.shape[1]
    tm, tn = _tile(s, 1024), 1024
    return pl.pallas_call(
        _residual_proj_kernel,
        grid=(n // tn, s // tm),
        in_specs=[pl.BlockSpec((tm, k), lambda j, i: (i, 0)),
                  pl.BlockSpec((k, tn), lambda j, i: (0, j)),
                  pl.BlockSpec((tm, tn), lambda j, i: (i, j))],
        out_specs=pl.BlockSpec((tm, tn), lambda j, i: (i, j)),
        out_shape=jax.ShapeDtypeStruct((s, n), F32),
        compiler_params=_params("arbitrary", "arbitrary"),
        name="out_proj",
    )(a, w, x)


def _ffn_up_kernel(h_ref, wg_ref, wv_ref, cwg_ref, cwv_ref, o_ref, cg_ref, cv_ref):
    j = pl.program_id(1)

    @pl.when(pl.program_id(0) == 0)
    def _():
        cg_ref[j] = jnp.zeros(cg_ref.shape[1:], F32)
        cv_ref[j] = jnp.zeros(cv_ref.shape[1:], F32)

    h = h_ref[...]
    ug = _causal_conv3(_dot(h, wg_ref[...]), cwg_ref[...], cg_ref.at[j])
    uv = _causal_conv3(_dot(h, wv_ref[...]), cwv_ref[...], cv_ref.at[j])
    o_ref[...] = (ug * jax.nn.sigmoid(ug) * uv).astype(o_ref.dtype)


def _ffn_up(h, wg, wv, cwg, cwv):
    s, d = h.shape
    f = wg.shape[1]
    tm, tn = _tile(s, 1024), 512
    nj = f // tn
    return pl.pallas_call(
        _ffn_up_kernel,
        grid=(s // tm, nj),
        in_specs=[pl.BlockSpec((tm, d), lambda i, j: (i, 0)),
                  pl.BlockSpec((d, tn), lambda i, j: (0, j)),
                  pl.BlockSpec((d, tn), lambda i, j: (0, j)),
                  pl.BlockSpec((3, tn), lambda i, j: (0, j)),
                  pl.BlockSpec((3, tn), lambda i, j: (0, j))],
        out_specs=pl.BlockSpec((tm, tn), lambda i, j: (i, j)),
        out_shape=jax.ShapeDtypeStruct((s, f), BF16),
        scratch_shapes=[pltpu.VMEM((nj, 8, tn), F32),
                        pltpu.VMEM((nj, 8, tn), F32)],
        compiler_params=_params("arbitrary", "arbitrary"),
        name="ffn_up",
    )(h, wg, wv, cwg, cwv)


def _ffn_down_kernel(a_ref, w_ref, x_ref, o_ref, ob_ref, acc_ref):
    k = pl.program_id(2)

    @pl.when(k == 0)
    def _():
        acc_ref[...] = x_ref[...]

    acc_ref[...] += _dot(a_ref[...], w_ref[...])

    @pl.when(k == pl.num_programs(2) - 1)
    def _():
        o_ref[...] = acc_ref[...]
        ob_ref[...] = acc_ref[...].astype(ob_ref.dtype)


def _ffn_down(act, w, x):
    s, f = act.shape
    n = w.shape[1]
    tm, tn = _tile(s, 1024), 1024
    tk = _tile(f, 2816)
    out = pl.BlockSpec((tm, tn), lambda i, j, k: (i, j))
    return pl.pallas_call(
        _ffn_down_kernel,
        grid=(s // tm, n // tn, f // tk),
        in_specs=[pl.BlockSpec((tm, tk), lambda i, j, k: (i, k)),
                  pl.BlockSpec((tk, tn), lambda i, j, k: (k, j)),
                  out],
        out_specs=[out, out],
        out_shape=[jax.ShapeDtypeStruct((s, n), F32),
                   jax.ShapeDtypeStruct((s, n), BF16)],
        scratch_shapes=[pltpu.VMEM((tm, tn), F32)],
        compiler_params=_params("arbitrary", "arbitrary", "arbitrary"),
        name="ffn_down",
    )(act, w, x)


def _ple_kernel(xb_ref, wpg_ref, p_ref, wp_ref, x_ref, o_ref):
    gate = jax.nn.sigmoid(_dot(xb_ref[...], wpg_ref[...]))
    emb = _dot(p_ref[...].astype(BF16), wp_ref[...])
    o_ref[...] = x_ref[...] + gate * emb


def _ple(xb, w_pg, p, w_p, x):
    s, d = xb.shape
    n = w_pg.shape[1]
    dp = p.shape[1]
    tm, tn = _tile(s, 1024), 512
    return pl.pallas_call(
        _ple_kernel,
        grid=(n // tn, s // tm),
        in_specs=[pl.BlockSpec((tm, d), lambda j, i: (i, 0)),
                  pl.BlockSpec((d, tn), lambda j, i: (0, j)),
                  pl.BlockSpec((tm, dp), lambda j, i: (i, 0)),
                  pl.BlockSpec((dp, tn), lambda j, i: (0, j)),
                  pl.BlockSpec((tm, tn), lambda j, i: (i, j))],
        out_specs=pl.BlockSpec((tm, tn), lambda j, i: (i, j)),
        out_shape=jax.ShapeDtypeStruct((s, n), F32),
        compiler_params=_params("arbitrary", "arbitrary"),
        name="ple",
    )(xb, w_pg, p, w_p, x)


def _pad_cols(a, n):
    return jnp.pad(a, ((0, 0), (0, n - a.shape[1])))


def _layer(x, p, cos, sin, w_norm_mix, w_in, q_norm, k_norm, conv_mix_w,
           w_attn_branch, w_conv_branch, w_out, w_norm_ffn, w_up, ffn_conv_w,
           w_down, w_ple, w_ple_gate):
    d_ff = w_down.shape[0]
    d_ff_pad = -(-d_ff // 2816) * 2816 if d_ff > 2816 else d_ff
    w_in_b = w_in.astype(BF16)

    gains = jnp.concatenate([jnp.tile(q_norm * (HEAD_DIM ** -0.5 * LOG2_E), N_ATTN_HEADS),
                             jnp.tile(k_norm, N_ATTN_HEADS)]).reshape(1, 2 * D_ATTN)

    h = _rmsnorm(x, w_norm_mix)
    qk = _qk_proj(h, w_in_b, gains, cos, sin)
    v = _plain_proj(h, w_in_b, 2 * D_ATTN, D_ATTN, sigmoid=False)
    conv = _conv_proj(h, w_in_b, conv_mix_w, 3 * D_ATTN)
    gates = _plain_proj(h, w_in_b, 3 * D_ATTN + 3 * D_CONV, 2 * x.shape[1],
                        sigmoid=True)
    attn = _moba_attention(qk, v)
    merged = _merge(attn, conv, w_attn_branch.astype(BF16),
                    w_conv_branch.astype(BF16), gates)
    x1 = _residual_proj(merged, w_out.astype(BF16), x)

    h2 = _rmsnorm(x1, w_norm_ffn)
    wg = _pad_cols(w_up[:, :d_ff].astype(BF16), d_ff_pad)
    wv = _pad_cols(w_up[:, d_ff:].astype(BF16), d_ff_pad)
    cwg = _pad_cols(ffn_conv_w[:, :d_ff], d_ff_pad)
    cwv = _pad_cols(ffn_conv_w[:, d_ff:], d_ff_pad)
    act = _ffn_up(h2, wg, wv, cwg, cwv)
    w_down_b = jnp.pad(w_down.astype(BF16), ((0, d_ff_pad - d_ff), (0, 0)))
    x2, x2b = _ffn_down(act, w_down_b, x1)

    return _ple(x2b, w_ple_gate.astype(BF16), p, w_ple.astype(BF16), x2)


def kernel(x, p, positions, w_norm_mix, w_in, q_norm, k_norm, conv_mix_w,
           w_attn_branch, w_conv_branch, w_out, w_norm_ffn, w_up, ffn_conv_w,
           w_down, w_ple, w_ple_gate):
    b, s, d = x.shape
    assert b == 1, "single-sequence prefill only"
    xs = x.reshape(s, d)
    cos, sin = _rope_tables(positions.reshape(s))
    for i in range(w_in.shape[0]):
        xs = _layer(xs, p[i].reshape(s, -1), cos, sin, w_norm_mix[i], w_in[i],
                    q_norm[i], k_norm[i], conv_mix_w[i], w_attn_branch[i],
                    w_conv_branch[i], w_out[i], w_norm_ffn[i], w_up[i],
                    ffn_conv_w[i], w_down[i], w_ple[i], w_ple_gate[i])
    return xs.reshape(b, s, d)
```

```python
import functools
import math

import jax
import jax.numpy as jnp
from jax import lax
from jax.experimental import pallas as pl
from jax.experimental.pallas import tpu as pltpu

N_ATTN_HEADS = 16
HEAD_DIM = 128
MOBA_BLOCK = 256
MOBA_TOPK = 3
ROPE_THETA = 10000.0
RMS_EPS = 1e-6
D_ATTN = N_ATTN_HEADS * HEAD_DIM
D_CONV = 2048

LANES = 128
MASK_VALUE = -1e30
LOG2_E = math.log2(math.e)
VMEM_LIMIT_BYTES = 56 * 1024 * 1024
BF16 = jnp.bfloat16
F32 = jnp.float32


def _params(*semantics):
    return pltpu.CompilerParams(dimension_semantics=semantics,
                                vmem_limit_bytes=VMEM_LIMIT_BYTES)


def _tile(n, preferred):
    return preferred if n % preferred == 0 else n


def _dot(a, b):
    return jnp.dot(a, b, preferred_element_type=F32)


def _dot_nt(a, b):
    return lax.dot_general(a, b, (((1,), (1,)), ((), ())),
                           preferred_element_type=F32)


def _rmsnorm_kernel(x_ref, g_ref, o_ref):
    x = x_ref[...]
    ms = jnp.mean(x * x, axis=-1, keepdims=True)
    o_ref[...] = (x * lax.rsqrt(ms + RMS_EPS) * g_ref[...]).astype(o_ref.dtype)


def _rmsnorm(x, g):
    s, d = x.shape
    tm = _tile(s, 256)
    return pl.pallas_call(
        _rmsnorm_kernel,
        grid=(s // tm,),
        in_specs=[pl.BlockSpec((tm, d), lambda i: (i, 0)),
                  pl.BlockSpec((1, d), lambda i: (0, 0))],
        out_specs=pl.BlockSpec((tm, d), lambda i: (i, 0)),
        out_shape=jax.ShapeDtypeStruct((s, d), BF16),
        compiler_params=_params("arbitrary"),
        name="rmsnorm",
    )(x, g.reshape(1, d))


def _rope_table_kernel(pos_ref, invf_ref, sign_ref, cos_ref, sin_ref):
    ang = pos_ref[...].astype(F32) * invf_ref[...]
    cos_ref[...] = jnp.cos(ang)
    sin_ref[...] = jnp.sin(ang) * sign_ref[...]


def _rope_tables(positions):
    s = positions.shape[0]
    half = HEAD_DIM // 2
    inv_freq = ROPE_THETA ** (-jnp.arange(half, dtype=F32) / half)
    invf = jnp.concatenate([inv_freq, inv_freq]).reshape(1, HEAD_DIM)
    sign = jnp.concatenate([-jnp.ones((half,), F32),
                            jnp.ones((half,), F32)]).reshape(1, HEAD_DIM)
    tm = _tile(s, 1024)
    row = pl.BlockSpec((1, HEAD_DIM), lambda i: (0, 0))
    tab = pl.BlockSpec((tm, HEAD_DIM), lambda i: (i, 0))
    return pl.pallas_call(
        _rope_table_kernel,
        grid=(s // tm,),
        in_specs=[pl.BlockSpec((tm, 1), lambda i: (i, 0)), row, row],
        out_specs=[tab, tab],
        out_shape=[jax.ShapeDtypeStruct((s, HEAD_DIM), F32)] * 2,
        compiler_params=_params("arbitrary"),
        name="rope_tables",
    )(positions.reshape(s, 1), invf, sign)


def _qk_proj_kernel(h_ref, w_ref, g_ref, cos_ref, sin_ref, o_ref):
    acc = _dot(h_ref[...], w_ref[...])
    cos = cos_ref[...]
    sin = sin_ref[...]
    for hd in range(acc.shape[1] // HEAD_DIM):
        cols = slice(hd * HEAD_DIM, (hd + 1) * HEAD_DIM)
        a = acc[:, cols]
        ms = jnp.mean(a * a, axis=-1, keepdims=True)
        y = a * lax.rsqrt(ms + RMS_EPS) * g_ref[:, cols]
        r = y * cos + pltpu.roll(y, HEAD_DIM // 2, axis=1) * sin
        o_ref[:, cols] = r.astype(o_ref.dtype)


def _plain_proj_kernel(h_ref, w_ref, o_ref, *, sigmoid):
    acc = _dot(h_ref[...], w_ref[...])
    if sigmoid:
        acc = jax.nn.sigmoid(acc)
    o_ref[...] = acc.astype(o_ref.dtype)


def _qk_proj(h, w_in, gains, cos, sin):
    s, d = h.shape
    n = 2 * D_ATTN
    tm, tn = _tile(s, 1024), 1024
    tab = pl.BlockSpec((tm, HEAD_DIM), lambda j, i: (i, 0))
    return pl.pallas_call(
        _qk_proj_kernel,
        grid=(n // tn, s // tm),
        in_specs=[pl.BlockSpec((tm, d), lambda j, i: (i, 0)),
                  pl.BlockSpec((d, tn), lambda j, i: (0, j)),
                  pl.BlockSpec((1, tn), lambda j, i: (0, j)),
                  tab, tab],
        out_specs=pl.BlockSpec((tm, tn), lambda j, i: (i, j)),
        out_shape=jax.ShapeDtypeStruct((s, n), BF16),
        compiler_params=_params("arbitrary", "arbitrary"),
        name="qk_proj",
    )(h, w_in, gains, cos, sin)


def _plain_proj(h, w_in, col0, n, *, sigmoid):
    s, d = h.shape
    tm, tn = _tile(s, 1024), 1024
    j0 = col0 // tn
    return pl.pallas_call(
        functools.partial(_plain_proj_kernel, sigmoid=sigmoid),
        grid=(n // tn, s // tm),
        in_specs=[pl.BlockSpec((tm, d), lambda j, i: (i, 0)),
                  pl.BlockSpec((d, tn), lambda j, i: (0, j0 + j))],
        out_specs=pl.BlockSpec((tm, tn), lambda j, i: (i, j)),
        out_shape=jax.ShapeDtypeStruct((s, n), BF16),
        compiler_params=_params("arbitrary", "arbitrary"),
        name="gate_proj" if sigmoid else "v_proj",
    )(h, w_in)


def _causal_conv3(m, w, carry_ref):
    tm = m.shape[0]
    rows = lax.broadcasted_iota(jnp.int32, m.shape, 0)
    prev = carry_ref[...]
    c2 = prev[6:7]
    c1 = prev[7:8]
    m1 = jnp.where(rows == 0, c1, pltpu.roll(m, 1, axis=0))
    m2 = jnp.where(rows == 0, c2,
                   jnp.where(rows == 1, c1, pltpu.roll(m, 2, axis=0)))
    carry_ref[...] = m[tm - 8:tm]
    return w[2:3] * m + w[0:1] * m2 + w[1:2] * m1


def _conv_proj_kernel(h_ref, wb_ref, wc_ref, wx_ref, cw_ref, o_ref, carry_ref):
    @pl.when(pl.program_id(1) == 0)
    def _():
        carry_ref[...] = jnp.zeros_like(carry_ref)

    h = h_ref[...]
    cb = _dot(h, wb_ref[...])
    m = _dot(h, wc_ref[...]) * _dot(h, wx_ref[...])
    y = _causal_conv3(m, cw_ref[...], carry_ref)
    o_ref[...] = (cb * y).astype(o_ref.dtype)


def _conv_proj(h, w_in, conv_w, col0):
    s, d = h.shape
    tm, tn = _tile(s, 1024), 256
    jb, jc, jx = (col0 // tn, (col0 + D_CONV) // tn, (col0 + 2 * D_CONV) // tn)
    return pl.pallas_call(
        _conv_proj_kernel,
        grid=(D_CONV // tn, s // tm),
        in_specs=[pl.BlockSpec((tm, d), lambda j, i: (i, 0)),
                  pl.BlockSpec((d, tn), lambda j, i: (0, jb + j)),
                  pl.BlockSpec((d, tn), lambda j, i: (0, jc + j)),
                  pl.BlockSpec((d, tn), lambda j, i: (0, jx + j)),
                  pl.BlockSpec((3, tn), lambda j, i: (0, j))],
        out_specs=pl.BlockSpec((tm, tn), lambda j, i: (i, j)),
        out_shape=jax.ShapeDtypeStruct((s, D_CONV), BF16),
        scratch_shapes=[pltpu.VMEM((8, tn), F32)],
        compiler_params=_params("arbitrary", "arbitrary"),
        name="conv_proj",
    )(h, w_in, w_in, w_in, conv_w)


TILE_BLOCKS = 2
ATTN_TILE = TILE_BLOCKS * MOBA_BLOCK
BF16_SUBLANES = 16
ONES_ROWS = BF16_SUBLANES
SCORES_AHEAD = 2


def _attn_kernel(qlo_ref, qhi_ref, k_ref, vt_ref, olo_ref, ohi_ref,
                 kaug_ref, qaug_ref, kmean_ref, acc_ref, m_ref, *, n_blocks):
    hd = pl.program_id(0)
    step = pl.program_id(1)
    blk, d, tile = MOBA_BLOCK, HEAD_DIM, ATTN_TILE
    n_q = n_blocks // TILE_BLOCKS
    gate_rows = kmean_ref.shape[0]
    q_tiles = (step, n_q - 1 - step)

    @pl.when((hd == 0) & (step == 0))
    def _():
        kmean_ref[...] = jnp.zeros_like(kmean_ref)
        lane = lax.broadcasted_iota(jnp.int32, (blk, d), 1)

        def body(b, carry):
            rows = pl.ds(pl.multiple_of(b * blk, blk), blk)
            kaug_ref[rows, d:2 * d] = (lane == b).astype(BF16)
            return carry
        lax.fori_loop(0, n_blocks, body, 0)
        if gate_rows < d:
            for w in range(2):
                qaug_ref[w, d + gate_rows:2 * d, :] = jnp.full(
                    (d - gate_rows, tile), MASK_VALUE, BF16)

    @pl.when(step == 0)
    def _():
        def body(b, carry):
            rows = pl.ds(pl.multiple_of(b * blk, blk), blk)
            kb = k_ref[rows, :]
            kaug_ref[rows, 0:d] = kb
            kmean_ref[pl.ds(b, 1), :] = (
                jnp.sum(kb.astype(F32), axis=0, keepdims=True) * (1.0 / blk))
            return carry
        lax.fori_loop(0, n_blocks, body, 0)

    blk_id = lax.broadcasted_iota(jnp.int32, (gate_rows, tile), 0)
    q_lane = lax.broadcasted_iota(jnp.int32, (gate_rows, tile), 1)
    kmean = kmean_ref[...].astype(BF16)
    for w, (q_ref, q_tile) in enumerate(zip((qlo_ref, qhi_ref), q_tiles)):
        qt = q_ref[...]
        own_blk = q_tile * TILE_BLOCKS + q_lane // blk
        gate = _dot(kmean, qt)
        gate = jnp.where(blk_id < own_blk, gate, -jnp.inf)
        sel = blk_id == own_blk
        for _ in range(MOBA_TOPK):
            mx = jnp.max(gate, axis=0, keepdims=True)
            idx = jnp.min(jnp.where(gate == mx, blk_id, gate_rows), axis=0,
                          keepdims=True)
            hit = blk_id == idx
            sel = sel | (hit & (mx > -jnp.inf))
            gate = jnp.where(hit, -jnp.inf, gate)
        qaug_ref[w, 0:d, :] = qt
        qaug_ref[w, d:d + gate_rows, :] = jnp.where(sel, 0.0, MASK_VALUE).astype(BF16)

    def scores(t, w):
        keys = pl.ds(pl.multiple_of(t * tile, tile), tile)
        return _dot(kaug_ref[keys, :], qaug_ref[w])

    def weighted_values(t, pb):
        out = _dot(vt_ref[0, t * TILE_BLOCKS], pb[0:blk])
        for u in range(1, TILE_BLOCKS):
            out += _dot(vt_ref[0, t * TILE_BLOCKS + u], pb[u * blk:(u + 1) * blk])
        return out

    tasks = [(q_tile, w) for w, q_tile in enumerate(q_tiles)]
    for slot in range(n_q - 1):
        tasks.append((jnp.where(slot < step, slot, slot - step),
                      (slot >= step).astype(jnp.int32)))
    key_pos = lax.broadcasted_iota(jnp.int32, (tile, tile), 0)
    q_pos = lax.broadcasted_iota(jnp.int32, (tile, tile), 1)

    queue = [scores(*tasks[n]) for n in range(SCORES_AHEAD)]
    for n, (t, w) in enumerate(tasks):
        s = queue.pop(0)
        if n + SCORES_AHEAD < len(tasks):
            queue.append(scores(*tasks[n + SCORES_AHEAD]))
        if n < 2:
            s = jnp.where(key_pos <= q_pos, s, MASK_VALUE)
            m_next = jnp.max(s, axis=0, keepdims=True)
        else:
            m_prev = m_ref[w, 0:1, :]
            m_next = jnp.maximum(m_prev, jnp.max(s, axis=0, keepdims=True))
        pv = weighted_values(t, jnp.exp2(s - m_next).astype(BF16))
        m_ref[w] = jnp.broadcast_to(m_next, m_ref.shape[1:])
        if n < 2:
            acc_ref[w] = pv
        else:
            acc_ref[w] = acc_ref[w] * jnp.exp2(m_prev - m_next) + pv

    for w, o_ref in enumerate((olo_ref, ohi_ref)):
        acc = acc_ref[w]
        o_ref[...] = (acc[0:d] / acc[d:d + 1]).T.astype(o_ref.dtype)


def _moba_attention(qk, v):
    s = v.shape[0]
    n_blocks = s // MOBA_BLOCK
    blk, d, tile = MOBA_BLOCK, HEAD_DIM, ATTN_TILE
    n_q = s // tile
    assert s % (2 * tile) == 0 and n_blocks <= LANES
    gate_rows = -(-n_blocks // BF16_SUBLANES) * BF16_SUBLANES
    qt = qk[:, :D_ATTN].T
    vt = v.reshape(n_blocks, blk, N_ATTN_HEADS, d).transpose(2, 0, 3, 1)
    vt = jnp.concatenate(
        [vt, jnp.ones((N_ATTN_HEADS, n_blocks, ONES_ROWS, blk), BF16)], axis=2)
    half = jax.ShapeDtypeStruct((s // 2, D_ATTN), BF16)
    lo, hi = pl.pallas_call(
        functools.partial(_attn_kernel, n_blocks=n_blocks),
        grid=(N_ATTN_HEADS, n_q // 2),
        in_specs=[pl.BlockSpec((d, tile), lambda h, i: (h, i)),
                  pl.BlockSpec((d, tile), lambda h, i: (h, n_q - 1 - i)),
                  pl.BlockSpec((s, d), lambda h, i: (0, N_ATTN_HEADS + h)),
                  pl.BlockSpec((1, n_blocks, d + ONES_ROWS, blk),
                               lambda h, i: (h, 0, 0, 0))],
        out_specs=[pl.BlockSpec((tile, d), lambda h, i: (i, h)),
                   pl.BlockSpec((tile, d), lambda h, i: (n_q // 2 - 1 - i, h))],
        out_shape=[half, half],
        scratch_shapes=[pltpu.VMEM((s, 2 * d), BF16),
                        pltpu.VMEM((2, 2 * d, tile), BF16),
                        pltpu.VMEM((gate_rows, d), F32),
                        pltpu.VMEM((2, d + ONES_ROWS, tile), F32),
                        pltpu.VMEM((2, 8, tile), F32)],
        compiler_params=_params("arbitrary", "arbitrary"),
        name="moba_attention",
    )(qt, qt, qk, vt)
    return jnp.concatenate([lo, hi], axis=0)


def _merge_kernel(a_ref, c_ref, wa_ref, wc_ref, ga_ref, gc_ref, o_ref):
    ya = _dot(a_ref[...], wa_ref[...])
    yc = _dot(c_ref[...], wc_ref[...])
    o_ref[...] = (ga_ref[...].astype(F32) * ya
                  + gc_ref[...].astype(F32) * yc).astype(o_ref.dtype)


def _merge(attn, conv, w_a, w_c, gates):
    s = attn.shape[0]
    n = w_a.shape[1]
    tm, tn = _tile(s, 1024), 1024
    jc = n // tn
    return pl.pallas_call(
        _merge_kernel,
        grid=(n // tn, s // tm),
        in_specs=[pl.BlockSpec((tm, D_ATTN), lambda j, i: (i, 0)),
                  pl.BlockSpec((tm, D_CONV), lambda j, i: (i, 0)),
                  pl.BlockSpec((D_ATTN, tn), lambda j, i: (0, j)),
                  pl.BlockSpec((D_CONV, tn), lambda j, i: (0, j)),
                  pl.BlockSpec((tm, tn), lambda j, i: (i, j)),
                  pl.BlockSpec((tm, tn), lambda j, i: (i, jc + j))],
        out_specs=pl.BlockSpec((tm, tn), lambda j, i: (i, j)),
        out_shape=jax.ShapeDtypeStruct((s, n), BF16),
        compiler_params=_params("arbitrary", "arbitrary"),
        name="merge",
    )(attn, conv, w_a, w_c, gates, gates)


def _residual_proj_kernel(a_ref, w_ref, x_ref, o_ref):
    o_ref[...] = x_ref[...] + _dot(a_ref[...], w_ref[...])


def _residual_proj(a, w, x):
    s, k = a.shape
    n = w.shape[1]
    tm, tn = _tile(s, 1024), 1024
    return pl.pallas_call(
        _residual_proj_kernel,
        grid=(n // tn, s // tm),
        in_specs=[pl.BlockSpec((tm, k), lambda j, i: (i, 0)),
                  pl.BlockSpec((k, tn), lambda j, i: (0, j)),
                  pl.BlockSpec((tm, tn), lambda j, i: (i, j))],
        out_specs=pl.BlockSpec((tm, tn), lambda j, i: (i, j)),
        out_shape=jax.ShapeDtypeStruct((s, n), F32),
        compiler_params=_params("arbitrary", "arbitrary"),
        name="out_proj",
    )(a, w, x)


FFN_UP_COLS = 512
FFN_UP_HALF = FFN_UP_COLS // 2


def _ffn_up_kernel(h_ref, wg_ref, wva_ref, wvb_ref, cwg_ref, cwva_ref, cwvb_ref,
                   o_ref, cg_ref, cv_ref):
    j = pl.program_id(1)

    @pl.when(pl.program_id(0) == 0)
    def _():
        cg_ref[j] = jnp.zeros(cg_ref.shape[1:], F32)
        cv_ref[j] = jnp.zeros(cv_ref.shape[1:], F32)

    h = h_ref[...]
    ug = _causal_conv3(_dot(h, wg_ref[...]), cwg_ref[...], cg_ref.at[j])
    uv = jnp.concatenate([_dot(h, wva_ref[...]), _dot(h, wvb_ref[...])], axis=1)
    cwv = jnp.concatenate([cwva_ref[...], cwvb_ref[...]], axis=1)
    uv = _causal_conv3(uv, cwv, cv_ref.at[j])
    o_ref[...] = (ug * jax.nn.sigmoid(ug) * uv).astype(o_ref.dtype)


def _ffn_up(h, w_up, conv_w):
    s, d = h.shape
    f = w_up.shape[1] // 2
    tm, tn, th = _tile(s, 1024), FFN_UP_COLS, FFN_UP_HALF
    assert f % th == 0
    nj = pl.cdiv(f, tn)
    v0 = f // th
    last = 2 * f // th - 1

    def g_map(i, j):
        return (0, j)

    def va_map(i, j):
        return (0, v0 + 2 * j)

    def vb_map(i, j):
        return (0, jnp.minimum(v0 + 2 * j + 1, last))

    return pl.pallas_call(
        _ffn_up_kernel,
        grid=(s // tm, nj),
        in_specs=[pl.BlockSpec((tm, d), lambda i, j: (i, 0)),
                  pl.BlockSpec((d, tn), g_map),
                  pl.BlockSpec((d, th), va_map),
                  pl.BlockSpec((d, th), vb_map),
                  pl.BlockSpec((3, tn), g_map),
                  pl.BlockSpec((3, th), va_map),
                  pl.BlockSpec((3, th), vb_map)],
        out_specs=pl.BlockSpec((tm, tn), lambda i, j: (i, j)),
        out_shape=jax.ShapeDtypeStruct((s, f), BF16),
        scratch_shapes=[pltpu.VMEM((nj, 8, tn), F32),
                        pltpu.VMEM((nj, 8, tn), F32)],
        compiler_params=_params("arbitrary", "arbitrary"),
        name="ffn_up",
    )(h, w_up, w_up, w_up, conv_w, conv_w, conv_w)


def _ffn_down_kernel(a_ref, w_ref, x_ref, o_ref, ob_ref, acc_ref):
    k = pl.program_id(2)

    @pl.when(k == 0)
    def _():
        acc_ref[...] = x_ref[...]

    acc_ref[...] += _dot(a_ref[...], w_ref[...])

    @pl.when(k == pl.num_programs(2) - 1)
    def _():
        o_ref[...] = acc_ref[...]
        ob_ref[...] = acc_ref[...].astype(ob_ref.dtype)


def _ffn_down(act, w, x):
    s, f = act.shape
    n = w.shape[1]
    tm, tn = _tile(s, 512), 1024
    tk = f // 2 if (f // 2) % LANES == 0 else f
    out = pl.BlockSpec((tm, tn), lambda i, j, k: (i, j))
    return pl.pallas_call(
        _ffn_down_kernel,
        grid=(s // tm, n // tn, f // tk),
        in_specs=[pl.BlockSpec((tm, tk), lambda i, j, k: (i, k)),
                  pl.BlockSpec((tk, tn), lambda i, j, k: (k, j)),
                  out],
        out_specs=[out, out],
        out_shape=[jax.ShapeDtypeStruct((s, n), F32),
                   jax.ShapeDtypeStruct((s, n), BF16)],
        scratch_shapes=[pltpu.VMEM((tm, tn), F32)],
        compiler_params=_params("arbitrary", "arbitrary", "arbitrary"),
        name="ffn_down",
    )(act, w, x)


def _ple_kernel(xb_ref, wpg_ref, p_ref, wp_ref, x_ref, o_ref):
    gate = jax.nn.sigmoid(_dot(xb_ref[...], wpg_ref[...]))
    emb = _dot(p_ref[...].astype(BF16), wp_ref[...])
    o_ref[...] = x_ref[...] + gate * emb


def _ple(xb, w_pg, p, w_p, x):
    s, d = xb.shape
    n = w_pg.shape[1]
    dp = p.shape[1]
    tm, tn = _tile(s, 1024), 512
    return pl.pallas_call(
        _ple_kernel,
        grid=(n // tn, s // tm),
        in_specs=[pl.BlockSpec((tm, d), lambda j, i: (i, 0)),
                  pl.BlockSpec((d, tn), lambda j, i: (0, j)),
                  pl.BlockSpec((tm, dp), lambda j, i: (i, 0)),
                  pl.BlockSpec((dp, tn), lambda j, i: (0, j)),
                  pl.BlockSpec((tm, tn), lambda j, i: (i, j))],
        out_specs=pl.BlockSpec((tm, tn), lambda j, i: (i, j)),
        out_shape=jax.ShapeDtypeStruct((s, n), F32),
        compiler_params=_params("arbitrary", "arbitrary"),
        name="ple",
    )(xb, w_pg, p, w_p, x)


def _layer(x, p, cos, sin, w_norm_mix, w_in, q_norm, k_norm, conv_mix_w,
           w_attn_branch, w_conv_branch, w_out, w_norm_ffn, w_up, ffn_conv_w,
           w_down, w_ple, w_ple_gate):
    w_in_b = w_in.astype(BF16)

    gains = jnp.concatenate([jnp.tile(q_norm * (HEAD_DIM ** -0.5 * LOG2_E), N_ATTN_HEADS),
                             jnp.tile(k_norm, N_ATTN_HEADS)]).reshape(1, 2 * D_ATTN)

    h = _rmsnorm(x, w_norm_mix)
    qk = _qk_proj(h, w_in_b, gains, cos, sin)
    v = _plain_proj(h, w_in_b, 2 * D_ATTN, D_ATTN, sigmoid=False)
    conv = _conv_proj(h, w_in_b, conv_mix_w, 3 * D_ATTN)
    gates = _plain_proj(h, w_in_b, 3 * D_ATTN + 3 * D_CONV, 2 * x.shape[1],
                        sigmoid=True)
    attn = _moba_attention(qk, v)
    merged = _merge(attn, conv, w_attn_branch.astype(BF16),
                    w_conv_branch.astype(BF16), gates)
    x1 = _residual_proj(merged, w_out.astype(BF16), x)

    h2 = _rmsnorm(x1, w_norm_ffn)
    act = _ffn_up(h2, w_up.astype(BF16), ffn_conv_w)
    x2, x2b = _ffn_down(act, w_down.astype(BF16), x1)

    return _ple(x2b, w_ple_gate.astype(BF16), p, w_ple.astype(BF16), x2)


def kernel(x, p, positions, w_norm_mix, w_in, q_norm, k_norm, conv_mix_w,
           w_attn_branch, w_conv_branch, w_out, w_norm_ffn, w_up, ffn_conv_w,
           w_down, w_ple, w_ple_gate):
    b, s, d = x.shape
    assert b == 1, "single-sequence prefill only"
    xs = x.reshape(s, d)
    cos, sin = _rope_tables(positions.reshape(s))
    for i in range(w_in.shape[0]):
        xs = _layer(xs, p[i].reshape(s, -1), cos, sin, w_norm_mix[i], w_in[i],
                    q_norm[i], k_norm[i], conv_mix_w[i], w_attn_branch[i],
                    w_conv_branch[i], w_out[i], w_norm_ffn[i], w_up[i],
                    ffn_conv_w[i], w_down[i], w_ple[i], w_ple_gate[i])
    return xs.reshape(b, s, d)
```

```python
import functools
import math

import jax
import jax.numpy as jnp
from jax import lax
from jax.experimental import pallas as pl
from jax.experimental.pallas import tpu as pltpu

N_ATTN_HEADS = 16
HEAD_DIM = 128
MOBA_BLOCK = 256
MOBA_TOPK = 3
ROPE_THETA = 10000.0
RMS_EPS = 1e-6
D_ATTN = N_ATTN_HEADS * HEAD_DIM
D_CONV = 2048

LANES = 128
MASK_VALUE = -1e30
LOG2_E = math.log2(math.e)
EPILOGUE_CHUNKS = 4
VMEM_LIMIT_BYTES = 56 * 1024 * 1024
BF16 = jnp.bfloat16
F32 = jnp.float32


def _params(*semantics):
    return pltpu.CompilerParams(dimension_semantics=semantics,
                                vmem_limit_bytes=VMEM_LIMIT_BYTES)


def _tile(n, preferred):
    return preferred if n % preferred == 0 else n


def _dot(a, b):
    return jnp.dot(a, b, preferred_element_type=F32)


def _dot_nt(a, b):
    return lax.dot_general(a, b, (((1,), (1,)), ((), ())),
                           preferred_element_type=F32)


def _rmsnorm_kernel(x_ref, g_ref, o_ref):
    x = x_ref[...]
    ms = jnp.mean(x * x, axis=-1, keepdims=True)
    o_ref[...] = (x * lax.rsqrt(ms + RMS_EPS) * g_ref[...]).astype(o_ref.dtype)


def _rmsnorm(x, g):
    s, d = x.shape
    tm = _tile(s, 256)
    return pl.pallas_call(
        _rmsnorm_kernel,
        grid=(s // tm,),
        in_specs=[pl.BlockSpec((tm, d), lambda i: (i, 0)),
                  pl.BlockSpec((1, d), lambda i: (0, 0))],
        out_specs=pl.BlockSpec((tm, d), lambda i: (i, 0)),
        out_shape=jax.ShapeDtypeStruct((s, d), BF16),
        compiler_params=_params("arbitrary"),
        name="rmsnorm",
    )(x, g.reshape(1, d))


def _rope_table_kernel(pos_ref, invf_ref, sign_ref, cos_ref, sin_ref):
    ang = pos_ref[...].astype(F32) * invf_ref[...]
    cos_ref[...] = jnp.cos(ang)
    sin_ref[...] = jnp.sin(ang) * sign_ref[...]


def _rope_tables(positions):
    s = positions.shape[0]
    half = HEAD_DIM // 2
    inv_freq = ROPE_THETA ** (-jnp.arange(half, dtype=F32) / half)
    invf = jnp.concatenate([inv_freq, inv_freq]).reshape(1, HEAD_DIM)
    sign = jnp.concatenate([-jnp.ones((half,), F32),
                            jnp.ones((half,), F32)]).reshape(1, HEAD_DIM)
    tm = _tile(s, 1024)
    row = pl.BlockSpec((1, HEAD_DIM), lambda i: (0, 0))
    tab = pl.BlockSpec((tm, HEAD_DIM), lambda i: (i, 0))
    return pl.pallas_call(
        _rope_table_kernel,
        grid=(s // tm,),
        in_specs=[pl.BlockSpec((tm, 1), lambda i: (i, 0)), row, row],
        out_specs=[tab, tab],
        out_shape=[jax.ShapeDtypeStruct((s, HEAD_DIM), F32)] * 2,
        compiler_params=_params("arbitrary"),
        name="rope_tables",
    )(positions.reshape(s, 1), invf, sign)


def _qk_proj_kernel(h_ref, w_ref, g_ref, cos_ref, sin_ref, o_ref):
    tm = h_ref.shape[0]
    chunk = tm // EPILOGUE_CHUNKS
    w = w_ref[...]
    accs = [_dot(h_ref[c * chunk:(c + 1) * chunk, :], w)
            for c in range(EPILOGUE_CHUNKS)]
    for c, acc in enumerate(accs):
        rows = slice(c * chunk, (c + 1) * chunk)
        cos = cos_ref[rows, :]
        sin = sin_ref[rows, :]
        for hd in range(acc.shape[1] // HEAD_DIM):
            cols = slice(hd * HEAD_DIM, (hd + 1) * HEAD_DIM)
            a = acc[:, cols]
            ms = jnp.mean(a * a, axis=-1, keepdims=True)
            y = a * lax.rsqrt(ms + RMS_EPS) * g_ref[:, cols]
            r = y * cos + pltpu.roll(y, HEAD_DIM // 2, axis=1) * sin
            o_ref[rows, cols] = r.astype(o_ref.dtype)


def _row_chunks(ref):
    chunk = ref.shape[0] // EPILOGUE_CHUNKS
    return [slice(c * chunk, (c + 1) * chunk) for c in range(EPILOGUE_CHUNKS)]


def _plain_proj_kernel(h_ref, w_ref, o_ref, *, sigmoid):
    w = w_ref[...]
    chunks = _row_chunks(h_ref)
    accs = [_dot(h_ref[rows, :], w) for rows in chunks]
    for rows, acc in zip(chunks, accs):
        if sigmoid:
            acc = jax.nn.sigmoid(acc)
        o_ref[rows, :] = acc.astype(o_ref.dtype)


def _qk_proj(h, w_in, gains, cos, sin):
    s, d = h.shape
    n = 2 * D_ATTN
    tm, tn = _tile(s, 1024), 1024
    tab = pl.BlockSpec((tm, HEAD_DIM), lambda j, i: (i, 0))
    return pl.pallas_call(
        _qk_proj_kernel,
        grid=(n // tn, s // tm),
        in_specs=[pl.BlockSpec((tm, d), lambda j, i: (i, 0)),
                  pl.BlockSpec((d, tn), lambda j, i: (0, j)),
                  pl.BlockSpec((1, tn), lambda j, i: (0, j)),
                  tab, tab],
        out_specs=pl.BlockSpec((tm, tn), lambda j, i: (i, j)),
        out_shape=jax.ShapeDtypeStruct((s, n), BF16),
        compiler_params=_params("arbitrary", "arbitrary"),
        name="qk_proj",
    )(h, w_in, gains, cos, sin)


def _plain_proj(h, w_in, col0, n, *, sigmoid):
    s, d = h.shape
    tm, tn = _tile(s, 1024), 1024
    j0 = col0 // tn
    return pl.pallas_call(
        functools.partial(_plain_proj_kernel, sigmoid=sigmoid),
        grid=(n // tn, s // tm),
        in_specs=[pl.BlockSpec((tm, d), lambda j, i: (i, 0)),
                  pl.BlockSpec((d, tn), lambda j, i: (0, j0 + j))],
        out_specs=pl.BlockSpec((tm, tn), lambda j, i: (i, j)),
        out_shape=jax.ShapeDtypeStruct((s, n), BF16),
        compiler_params=_params("arbitrary", "arbitrary"),
        name="gate_proj" if sigmoid else "v_proj",
    )(h, w_in)


def _causal_conv3(m, w, carry_ref):
    tm = m.shape[0]
    rows = lax.broadcasted_iota(jnp.int32, m.shape, 0)
    prev = carry_ref[...]
    c2 = prev[6:7]
    c1 = prev[7:8]
    m1 = jnp.where(rows == 0, c1, pltpu.roll(m, 1, axis=0))
    m2 = jnp.where(rows == 0, c2,
                   jnp.where(rows == 1, c1, pltpu.roll(m, 2, axis=0)))
    carry_ref[...] = m[tm - 8:tm]
    return w[2:3] * m + w[0:1] * m2 + w[1:2] * m1


def _conv_proj_kernel(h_ref, wb_ref, wc_ref, wx_ref, cw_ref, o_ref, carry_ref):
    @pl.when(pl.program_id(1) == 0)
    def _():
        carry_ref[...] = jnp.zeros_like(carry_ref)

    wb, wc, wx, cw = wb_ref[...], wc_ref[...], wx_ref[...], cw_ref[...]
    chunks = _row_chunks(h_ref)
    raw = []
    for rows in chunks:
        h = h_ref[rows, :]
        raw.append((_dot(h, wb), _dot(h, wc), _dot(h, wx)))
    for rows, (cb, cc, cx) in zip(chunks, raw):
        y = _causal_conv3(cc * cx, cw, carry_ref)
        o_ref[rows, :] = (cb * y).astype(o_ref.dtype)


def _conv_proj(h, w_in, conv_w, col0):
    s, d = h.shape
    tm, tn = _tile(s, 1024), 256
    jb, jc, jx = (col0 // tn, (col0 + D_CONV) // tn, (col0 + 2 * D_CONV) // tn)
    return pl.pallas_call(
        _conv_proj_kernel,
        grid=(D_CONV // tn, s // tm),
        in_specs=[pl.BlockSpec((tm, d), lambda j, i: (i, 0)),
                  pl.BlockSpec((d, tn), lambda j, i: (0, jb + j)),
                  pl.BlockSpec((d, tn), lambda j, i: (0, jc + j)),
                  pl.BlockSpec((d, tn), lambda j, i: (0, jx + j)),
                  pl.BlockSpec((3, tn), lambda j, i: (0, j))],
        out_specs=pl.BlockSpec((tm, tn), lambda j, i: (i, j)),
        out_shape=jax.ShapeDtypeStruct((s, D_CONV), BF16),
        scratch_shapes=[pltpu.VMEM((8, tn), F32)],
        compiler_params=_params("arbitrary", "arbitrary"),
        name="conv_proj",
    )(h, w_in, w_in, w_in, conv_w)


TILE_BLOCKS = 2
ATTN_TILE = TILE_BLOCKS * MOBA_BLOCK
BF16_SUBLANES = 16
ONES_ROWS = BF16_SUBLANES
SCORES_AHEAD = 2


def _attn_kernel(qlo_ref, qhi_ref, k_ref, vt_ref, olo_ref, ohi_ref,
                 kaug_ref, qaug_ref, kmean_ref, acc_ref, m_ref, *, n_blocks):
    hd = pl.program_id(0)
    step = pl.program_id(1)
    blk, d, tile = MOBA_BLOCK, HEAD_DIM, ATTN_TILE
    n_q = n_blocks // TILE_BLOCKS
    gate_rows = kmean_ref.shape[0]
    q_tiles = (step, n_q - 1 - step)

    @pl.when((hd == 0) & (step == 0))
    def _():
        kmean_ref[...] = jnp.zeros_like(kmean_ref)
        lane = lax.broadcasted_iota(jnp.int32, (blk, d), 1)

        def body(b, carry):
            rows = pl.ds(pl.multiple_of(b * blk, blk), blk)
            kaug_ref[rows, d:2 * d] = (lane == b).astype(BF16)
            return carry
        lax.fori_loop(0, n_blocks, body, 0)
        if gate_rows < d:
            for w in range(2):
                qaug_ref[w, d + gate_rows:2 * d, :] = jnp.full(
                    (d - gate_rows, tile), MASK_VALUE, BF16)

    @pl.when(step == 0)
    def _():
        def body(b, carry):
            rows = pl.ds(pl.multiple_of(b * blk, blk), blk)
            kb = k_ref[rows, :]
            kaug_ref[rows, 0:d] = kb
            kmean_ref[pl.ds(b, 1), :] = (
                jnp.sum(kb.astype(F32), axis=0, keepdims=True) * (1.0 / blk))
            return carry
        lax.fori_loop(0, n_blocks, body, 0)

    blk_id = lax.broadcasted_iota(jnp.int32, (gate_rows, tile), 0)
    q_lane = lax.broadcasted_iota(jnp.int32, (gate_rows, tile), 1)
    kmean = kmean_ref[...].astype(BF16)
    for w, (q_ref, q_tile) in enumerate(zip((qlo_ref, qhi_ref), q_tiles)):
        qt = q_ref[...]
        own_blk = q_tile * TILE_BLOCKS + q_lane // blk
        gate = _dot(kmean, qt)
        gate = jnp.where(blk_id < own_blk, gate, -jnp.inf)
        sel = blk_id == own_blk
        for _ in range(MOBA_TOPK):
            mx = jnp.max(gate, axis=0, keepdims=True)
            idx = jnp.min(jnp.where(gate == mx, blk_id, gate_rows), axis=0,
                          keepdims=True)
            hit = blk_id == idx
            sel = sel | (hit & (mx > -jnp.inf))
            gate = jnp.where(hit, -jnp.inf, gate)
        qaug_ref[w, 0:d, :] = qt
        qaug_ref[w, d:d + gate_rows, :] = jnp.where(sel, 0.0, MASK_VALUE).astype(BF16)

    def scores(t, w):
        keys = pl.ds(pl.multiple_of(t * tile, tile), tile)
        return _dot(kaug_ref[keys, :], qaug_ref[w])

    def weighted_values(t, pb):
        out = _dot(vt_ref[0, t * TILE_BLOCKS], pb[0:blk])
        for u in range(1, TILE_BLOCKS):
            out += _dot(vt_ref[0, t * TILE_BLOCKS + u], pb[u * blk:(u + 1) * blk])
        return out

    tasks = [(q_tile, w) for w, q_tile in enumerate(q_tiles)]
    for slot in range(n_q - 1):
        tasks.append((jnp.where(slot < step, slot, slot - step),
                      (slot >= step).astype(jnp.int32)))
    key_pos = lax.broadcasted_iota(jnp.int32, (tile, tile), 0)
    q_pos = lax.broadcasted_iota(jnp.int32, (tile, tile), 1)

    queue = [scores(*tasks[n]) for n in range(SCORES_AHEAD)]
    for n, (t, w) in enumerate(tasks):
        s = queue.pop(0)
        if n + SCORES_AHEAD < len(tasks):
            queue.append(scores(*tasks[n + SCORES_AHEAD]))
        if n < 2:
            s = jnp.where(key_pos <= q_pos, s, MASK_VALUE)
            m_next = jnp.max(s, axis=0, keepdims=True)
        else:
            m_prev = m_ref[w, 0:1, :]
            m_next = jnp.maximum(m_prev, jnp.max(s, axis=0, keepdims=True))
        pv = weighted_values(t, jnp.exp2(s - m_next).astype(BF16))
        m_ref[w] = jnp.broadcast_to(m_next, m_ref.shape[1:])
        if n < 2:
            acc_ref[w] = pv
        else:
            acc_ref[w] = acc_ref[w] * jnp.exp2(m_prev - m_next) + pv

    for w, o_ref in enumerate((olo_ref, ohi_ref)):
        acc = acc_ref[w]
        o_ref[...] = (acc[0:d] / acc[d:d + 1]).T.astype(o_ref.dtype)


def _moba_attention(qk, v):
    s = v.shape[0]
    n_blocks = s // MOBA_BLOCK
    blk, d, tile = MOBA_BLOCK, HEAD_DIM, ATTN_TILE
    n_q = s // tile
    assert s % (2 * tile) == 0 and n_blocks <= LANES
    gate_rows = -(-n_blocks // BF16_SUBLANES) * BF16_SUBLANES
    qt = qk[:, :D_ATTN].T
    vt = v.reshape(n_blocks, blk, N_ATTN_HEADS, d).transpose(2, 0, 3, 1)
    vt = jnp.concatenate(
        [vt, jnp.ones((N_ATTN_HEADS, n_blocks, ONES_ROWS, blk), BF16)], axis=2)
    half = jax.ShapeDtypeStruct((s // 2, D_ATTN), BF16)
    lo, hi = pl.pallas_call(
        functools.partial(_attn_kernel, n_blocks=n_blocks),
        grid=(N_ATTN_HEADS, n_q // 2),
        in_specs=[pl.BlockSpec((d, tile), lambda h, i: (h, i)),
                  pl.BlockSpec((d, tile), lambda h, i: (h, n_q - 1 - i)),
                  pl.BlockSpec((s, d), lambda h, i: (0, N_ATTN_HEADS + h)),
                  pl.BlockSpec((1, n_blocks, d + ONES_ROWS, blk),
                               lambda h, i: (h, 0, 0, 0))],
        out_specs=[pl.BlockSpec((tile, d), lambda h, i: (i, h)),
                   pl.BlockSpec((tile, d), lambda h, i: (n_q // 2 - 1 - i, h))],
        out_shape=[half, half],
        scratch_shapes=[pltpu.VMEM((s, 2 * d), BF16),
                        pltpu.VMEM((2, 2 * d, tile), BF16),
                        pltpu.VMEM((gate_rows, d), F32),
                        pltpu.VMEM((2, d + ONES_ROWS, tile), F32),
                        pltpu.VMEM((2, 8, tile), F32)],
        compiler_params=_params("arbitrary", "arbitrary"),
        name="moba_attention",
    )(qt, qt, qk, vt)
    return jnp.concatenate([lo, hi], axis=0)


def _merge_kernel(a_ref, c_ref, wa_ref, wc_ref, ga_ref, gc_ref, o_ref):
    wa, wc = wa_ref[...], wc_ref[...]
    chunks = _row_chunks(a_ref)
    raw = [(_dot(a_ref[rows, :], wa), _dot(c_ref[rows, :], wc)) for rows in chunks]
    for rows, (ya, yc) in zip(chunks, raw):
        o_ref[rows, :] = (ga_ref[rows, :].astype(F32) * ya
                          + gc_ref[rows, :].astype(F32) * yc).astype(o_ref.dtype)


def _merge(attn, conv, w_a, w_c, gates):
    s = attn.shape[0]
    n = w_a.shape[1]
    tm, tn = _tile(s, 1024), 1024
    jc = n // tn
    return pl.pallas_call(
        _merge_kernel,
        grid=(n // tn, s // tm),
        in_specs=[pl.BlockSpec((tm, D_ATTN), lambda j, i: (i, 0)),
                  pl.BlockSpec((tm, D_CONV), lambda j, i: (i, 0)),
                  pl.BlockSpec((D_ATTN, tn), lambda j, i: (0, j)),
                  pl.BlockSpec((D_CONV, tn), lambda j, i: (0, j)),
                  pl.BlockSpec((tm, tn), lambda j, i: (i, j)),
                  pl.BlockSpec((tm, tn), lambda j, i: (i, jc + j))],
        out_specs=pl.BlockSpec((tm, tn), lambda j, i: (i, j)),
        out_shape=jax.ShapeDtypeStruct((s, n), BF16),
        compiler_params=_params("arbitrary", "arbitrary"),
        name="merge",
    )(attn, conv, w_a, w_c, gates, gates)


def _residual_proj_kernel(a_ref, w_ref, x_ref, o_ref):
    o_ref[...] = x_ref[...] + _dot(a_ref[...], w_ref[...])


def _residual_proj(a, w, x):
    s, k = a.shape
    n = w.shape[1]
    tm, tn = _tile(s, 1024), 1024
    return pl.pallas_call(
        _residual_proj_kernel,
        grid=(n // tn, s // tm),
        in_specs=[pl.BlockSpec((tm, k), lambda j, i: (i, 0)),
                  pl.BlockSpec((k, tn), lambda j, i: (0, j)),
                  pl.BlockSpec((tm, tn), lambda j, i: (i, j))],
        out_specs=pl.BlockSpec((tm, tn), lambda j, i: (i, j)),
        out_shape=jax.ShapeDtypeStruct((s, n), F32),
        compiler_params=_params("arbitrary", "arbitrary"),
        name="out_proj",
    )(a, w, x)


FFN_UP_COLS = 512
FFN_UP_HALF = FFN_UP_COLS // 2


def _ffn_up_kernel(h_ref, wg_ref, wva_ref, wvb_ref, cwg_ref, cwva_ref, cwvb_ref,
                   o_ref, cg_ref, cv_ref):
    j = pl.program_id(1)

    @pl.when(pl.program_id(0) == 0)
    def _():
        cg_ref[j] = jnp.zeros(cg_ref.shape[1:], F32)
        cv_ref[j] = jnp.zeros(cv_ref.shape[1:], F32)

    chunk = h_ref.shape[0] // EPILOGUE_CHUNKS
    wg, wva, wvb = wg_ref[...], wva_ref[...], wvb_ref[...]
    cwg = cwg_ref[...]
    cwv = jnp.concatenate([cwva_ref[...], cwvb_ref[...]], axis=1)
    raw = []
    for c in range(EPILOGUE_CHUNKS):
        h = h_ref[c * chunk:(c + 1) * chunk, :]
        raw.append((_dot(h, wg),
                    jnp.concatenate([_dot(h, wva), _dot(h, wvb)], axis=1)))
    for c, (ug, uv) in enumerate(raw):
        ug = _causal_conv3(ug, cwg, cg_ref.at[j])
        uv = _causal_conv3(uv, cwv, cv_ref.at[j])
        o_ref[c * chunk:(c + 1) * chunk, :] = (
            ug * jax.nn.sigmoid(ug) * uv).astype(o_ref.dtype)


def _ffn_up(h, w_up, conv_w):
    s, d = h.shape
    f = w_up.shape[1] // 2
    tm, tn, th = _tile(s, 1024), FFN_UP_COLS, FFN_UP_HALF
    assert f % th == 0
    nj = pl.cdiv(f, tn)
    v0 = f // th
    last = 2 * f // th - 1

    def g_map(i, j):
        return (0, j)

    def va_map(i, j):
        return (0, v0 + 2 * j)

    def vb_map(i, j):
        return (0, jnp.minimum(v0 + 2 * j + 1, last))

    return pl.pallas_call(
        _ffn_up_kernel,
        grid=(s // tm, nj),
        in_specs=[pl.BlockSpec((tm, d), lambda i, j: (i, 0)),
                  pl.BlockSpec((d, tn), g_map),
                  pl.BlockSpec((d, th), va_map),
                  pl.BlockSpec((d, th), vb_map),
                  pl.BlockSpec((3, tn), g_map),
                  pl.BlockSpec((3, th), va_map),
                  pl.BlockSpec((3, th), vb_map)],
        out_specs=pl.BlockSpec((tm, tn), lambda i, j: (i, j)),
        out_shape=jax.ShapeDtypeStruct((s, f), BF16),
        scratch_shapes=[pltpu.VMEM((nj, 8, tn), F32),
                        pltpu.VMEM((nj, 8, tn), F32)],
        compiler_params=_params("arbitrary", "arbitrary"),
        name="ffn_up",
    )(h, w_up, w_up, w_up, conv_w, conv_w, conv_w)


def _ffn_down_kernel(a_ref, w_ref, x_ref, o_ref, ob_ref, acc_ref, *, n_k):
    k = pl.program_id(2)

    @pl.when(k == 0)
    def _():
        acc_ref[...] = x_ref[...] + _dot(a_ref[...], w_ref[...])

    if n_k > 2:
        @pl.when((k > 0) & (k < n_k - 1))
        def _():
            acc_ref[...] += _dot(a_ref[...], w_ref[...])

    @pl.when(k == n_k - 1)
    def _():
        out = acc_ref[...] + _dot(a_ref[...], w_ref[...])
        o_ref[...] = out
        ob_ref[...] = out.astype(ob_ref.dtype)


def _ffn_down(act, w, x):
    s, f = act.shape
    n = w.shape[1]
    tm, tn = _tile(s, 512), 1024
    tk = f // 2 if (f // 2) % LANES == 0 else f
    out = pl.BlockSpec((tm, tn), lambda i, j, k: (i, j))
    assert f // tk >= 2
    return pl.pallas_call(
        functools.partial(_ffn_down_kernel, n_k=f // tk),
        grid=(s // tm, n // tn, f // tk),
        in_specs=[pl.BlockSpec((tm, tk), lambda i, j, k: (i, k)),
                  pl.BlockSpec((tk, tn), lambda i, j, k: (k, j)),
                  out],
        out_specs=[out, out],
        out_shape=[jax.ShapeDtypeStruct((s, n), F32),
                   jax.ShapeDtypeStruct((s, n), BF16)],
        scratch_shapes=[pltpu.VMEM((tm, tn), F32)],
        compiler_params=_params("arbitrary", "arbitrary", "arbitrary"),
        name="ffn_down",
    )(act, w, x)


def _ple_kernel(xb_ref, wpg_ref, p_ref, wp_ref, x_ref, o_ref):
    wpg, wp = wpg_ref[...], wp_ref[...]
    chunks = _row_chunks(xb_ref)
    raw = [(_dot(xb_ref[rows, :], wpg), _dot(p_ref[rows, :].astype(BF16), wp))
           for rows in chunks]
    for rows, (gate, emb) in zip(chunks, raw):
        o_ref[rows, :] = x_ref[rows, :] + jax.nn.sigmoid(gate) * emb


def _ple(xb, w_pg, p, w_p, x):
    s, d = xb.shape
    n = w_pg.shape[1]
    dp = p.shape[1]
    tm, tn = _tile(s, 1024), 512
    return pl.pallas_call(
        _ple_kernel,
        grid=(n // tn, s // tm),
        in_specs=[pl.BlockSpec((tm, d), lambda j, i: (i, 0)),
                  pl.BlockSpec((d, tn), lambda j, i: (0, j)),
                  pl.BlockSpec((tm, dp), lambda j, i: (i, 0)),
                  pl.BlockSpec((dp, tn), lambda j, i: (0, j)),
                  pl.BlockSpec((tm, tn), lambda j, i: (i, j))],
        out_specs=pl.BlockSpec((tm, tn), lambda j, i: (i, j)),
        out_shape=jax.ShapeDtypeStruct((s, n), F32),
        compiler_params=_params("arbitrary", "arbitrary"),
        name="ple",
    )(xb, w_pg, p, w_p, x)


def _layer(x, p, cos, sin, w_norm_mix, w_in, q_norm, k_norm, conv_mix_w,
           w_attn_branch, w_conv_branch, w_out, w_norm_ffn, w_up, ffn_conv_w,
           w_down, w_ple, w_ple_gate):
    w_in_b = w_in.astype(BF16)

    gains = jnp.concatenate([jnp.tile(q_norm * (HEAD_DIM ** -0.5 * LOG2_E), N_ATTN_HEADS),
                             jnp.tile(k_norm, N_ATTN_HEADS)]).reshape(1, 2 * D_ATTN)

    h = _rmsnorm(x, w_norm_mix)
    qk = _qk_proj(h, w_in_b, gains, cos, sin)
    v = _plain_proj(h, w_in_b, 2 * D_ATTN, D_ATTN, sigmoid=False)
    conv = _conv_proj(h, w_in_b, conv_mix_w, 3 * D_ATTN)
    gates = _plain_proj(h, w_in_b, 3 * D_ATTN + 3 * D_CONV, 2 * x.shape[1],
                        sigmoid=True)
    attn = _moba_attention(qk, v)
    merged = _merge(attn, conv, w_attn_branch.astype(BF16),
                    w_conv_branch.astype(BF16), gates)
    x1 = _residual_proj(merged, w_out.astype(BF16), x)

    h2 = _rmsnorm(x1, w_norm_ffn)
    act = _ffn_up(h2, w_up.astype(BF16), ffn_conv_w)
    x2, x2b = _ffn_down(act, w_down.astype(BF16), x1)

    return _ple(x2b, w_ple_gate.astype(BF16), p, w_ple.astype(BF16), x2)


def kernel(x, p, positions, w_norm_mix, w_in, q_norm, k_norm, conv_mix_w,
           w_attn_branch, w_conv_branch, w_out, w_norm_ffn, w_up, ffn_conv_w,
           w_down, w_ple, w_ple_gate):
    b, s, d = x.shape
    assert b == 1, "single-sequence prefill only"
    xs = x.reshape(s, d)
    cos, sin = _rope_tables(positions.reshape(s))
    for i in range(w_in.shape[0]):
        xs = _layer(xs, p[i].reshape(s, -1), cos, sin, w_norm_mix[i], w_in[i],
                    q_norm[i], k_norm[i], conv_mix_w[i], w_attn_branch[i],
                    w_conv_branch[i], w_out[i], w_norm_ffn[i], w_up[i],
                    ffn_conv_w[i], w_down[i], w_ple[i], w_ple_gate[i])
    return xs.reshape(b, s, d)
```

```python
import functools
import math

import jax
import jax.numpy as jnp
from jax import lax
from jax.experimental import pallas as pl
from jax.experimental.pallas import tpu as pltpu

N_ATTN_HEADS = 16
HEAD_DIM = 128
MOBA_BLOCK = 256
MOBA_TOPK = 3
ROPE_THETA = 10000.0
RMS_EPS = 1e-6
D_ATTN = N_ATTN_HEADS * HEAD_DIM
D_CONV = 2048

LANES = 128
MASK_VALUE = -1e30
LOG2_E = math.log2(math.e)
EPILOGUE_CHUNKS = 4
VMEM_LIMIT_BYTES = 56 * 1024 * 1024
BF16 = jnp.bfloat16
F32 = jnp.float32


def _params(*semantics):
    return pltpu.CompilerParams(dimension_semantics=semantics,
                                vmem_limit_bytes=VMEM_LIMIT_BYTES)


def _tile(n, preferred):
    return preferred if n % preferred == 0 else n


def _dot(a, b):
    return jnp.dot(a, b, preferred_element_type=F32)


def _dot_nt(a, b):
    return lax.dot_general(a, b, (((1,), (1,)), ((), ())),
                           preferred_element_type=F32)


def _rmsnorm_kernel(x_ref, g_ref, o_ref):
    x = x_ref[...]
    ms = jnp.mean(x * x, axis=-1, keepdims=True)
    o_ref[...] = (x * lax.rsqrt(ms + RMS_EPS) * g_ref[...]).astype(o_ref.dtype)


def _rmsnorm(x, g):
    s, d = x.shape
    tm = _tile(s, 256)
    return pl.pallas_call(
        _rmsnorm_kernel,
        grid=(s // tm,),
        in_specs=[pl.BlockSpec((tm, d), lambda i: (i, 0)),
                  pl.BlockSpec((1, d), lambda i: (0, 0))],
        out_specs=pl.BlockSpec((tm, d), lambda i: (i, 0)),
        out_shape=jax.ShapeDtypeStruct((s, d), BF16),
        compiler_params=_params("arbitrary"),
        name="rmsnorm",
    )(x, g.reshape(1, d))


def _rope_table_kernel(pos_ref, invf_ref, sign_ref, cos_ref, sin_ref):
    ang = pos_ref[...].astype(F32) * invf_ref[...]
    cos_ref[...] = jnp.cos(ang)
    sin_ref[...] = jnp.sin(ang) * sign_ref[...]


def _rope_tables(positions):
    s = positions.shape[0]
    half = HEAD_DIM // 2
    inv_freq = ROPE_THETA ** (-jnp.arange(half, dtype=F32) / half)
    invf = jnp.concatenate([inv_freq, inv_freq]).reshape(1, HEAD_DIM)
    sign = jnp.concatenate([-jnp.ones((half,), F32),
                            jnp.ones((half,), F32)]).reshape(1, HEAD_DIM)
    tm = _tile(s, 1024)
    row = pl.BlockSpec((1, HEAD_DIM), lambda i: (0, 0))
    tab = pl.BlockSpec((tm, HEAD_DIM), lambda i: (i, 0))
    return pl.pallas_call(
        _rope_table_kernel,
        grid=(s // tm,),
        in_specs=[pl.BlockSpec((tm, 1), lambda i: (i, 0)), row, row],
        out_specs=[tab, tab],
        out_shape=[jax.ShapeDtypeStruct((s, HEAD_DIM), F32)] * 2,
        compiler_params=_params("arbitrary"),
        name="rope_tables",
    )(positions.reshape(s, 1), invf, sign)


def _qk_proj_kernel(h_ref, w_ref, g_ref, cos_ref, sin_ref, o_ref):
    tm = h_ref.shape[0]
    chunk = tm // EPILOGUE_CHUNKS
    w = w_ref[...]
    accs = [_dot(h_ref[c * chunk:(c + 1) * chunk, :], w)
            for c in range(EPILOGUE_CHUNKS)]
    for c, acc in enumerate(accs):
        rows = slice(c * chunk, (c + 1) * chunk)
        cos = cos_ref[rows, :]
        sin = sin_ref[rows, :]
        for hd in range(acc.shape[1] // HEAD_DIM):
            cols = slice(hd * HEAD_DIM, (hd + 1) * HEAD_DIM)
            a = acc[:, cols]
            ms = jnp.mean(a * a, axis=-1, keepdims=True)
            y = a * lax.rsqrt(ms + RMS_EPS) * g_ref[:, cols]
            r = y * cos + pltpu.roll(y, HEAD_DIM // 2, axis=1) * sin
            o_ref[rows, cols] = r.astype(o_ref.dtype)


def _row_chunks(ref):
    chunk = ref.shape[0] // EPILOGUE_CHUNKS
    return [slice(c * chunk, (c + 1) * chunk) for c in range(EPILOGUE_CHUNKS)]


def _plain_proj_kernel(h_ref, w_ref, o_ref, *, sigmoid):
    w = w_ref[...]
    chunks = _row_chunks(h_ref)
    accs = [_dot(h_ref[rows, :], w) for rows in chunks]
    for rows, acc in zip(chunks, accs):
        if sigmoid:
            acc = jax.nn.sigmoid(acc)
        o_ref[rows, :] = acc.astype(o_ref.dtype)


def _qk_proj(h, w_in, gains, cos, sin):
    s, d = h.shape
    n = 2 * D_ATTN
    tm, tn = _tile(s, 1024), 1024
    tab = pl.BlockSpec((tm, HEAD_DIM), lambda j, i: (i, 0))
    return pl.pallas_call(
        _qk_proj_kernel,
        grid=(n // tn, s // tm),
        in_specs=[pl.BlockSpec((tm, d), lambda j, i: (i, 0)),
                  pl.BlockSpec((d, tn), lambda j, i: (0, j)),
                  pl.BlockSpec((1, tn), lambda j, i: (0, j)),
                  tab, tab],
        out_specs=pl.BlockSpec((tm, tn), lambda j, i: (i, j)),
        out_shape=jax.ShapeDtypeStruct((s, n), BF16),
        compiler_params=_params("arbitrary", "arbitrary"),
        name="qk_proj",
    )(h, w_in, gains, cos, sin)


def _plain_proj(h, w_in, col0, n, *, sigmoid):
    s, d = h.shape
    tm, tn = _tile(s, 1024), 1024
    j0 = col0 // tn
    return pl.pallas_call(
        functools.partial(_plain_proj_kernel, sigmoid=sigmoid),
        grid=(n // tn, s // tm),
        in_specs=[pl.BlockSpec((tm, d), lambda j, i: (i, 0)),
                  pl.BlockSpec((d, tn), lambda j, i: (0, j0 + j))],
        out_specs=pl.BlockSpec((tm, tn), lambda j, i: (i, j)),
        out_shape=jax.ShapeDtypeStruct((s, n), BF16),
        compiler_params=_params("arbitrary", "arbitrary"),
        name="gate_proj" if sigmoid else "v_proj",
    )(h, w_in)


def _causal_conv3(m, w, carry_ref):
    tm = m.shape[0]
    rows = lax.broadcasted_iota(jnp.int32, m.shape, 0)
    prev = carry_ref[...]
    c2 = prev[6:7]
    c1 = prev[7:8]
    m1 = jnp.where(rows == 0, c1, pltpu.roll(m, 1, axis=0))
    m2 = jnp.where(rows == 0, c2,
                   jnp.where(rows == 1, c1, pltpu.roll(m, 2, axis=0)))
    carry_ref[...] = m[tm - 8:tm]
    return w[2:3] * m + w[0:1] * m2 + w[1:2] * m1


def _conv_proj_kernel(h_ref, wb_ref, wc_ref, wx_ref, cw_ref, o_ref, carry_ref):
    @pl.when(pl.program_id(1) == 0)
    def _():
        carry_ref[...] = jnp.zeros_like(carry_ref)

    h = h_ref[...]
    cb = _dot(h, wb_ref[...])
    m = _dot(h, wc_ref[...]) * _dot(h, wx_ref[...])
    y = _causal_conv3(m, cw_ref[...], carry_ref)
    o_ref[...] = (cb * y).astype(o_ref.dtype)


def _conv_proj(h, w_in, conv_w, col0):
    s, d = h.shape
    tm, tn = _tile(s, 1024), 256
    jb, jc, jx = (col0 // tn, (col0 + D_CONV) // tn, (col0 + 2 * D_CONV) // tn)
    return pl.pallas_call(
        _conv_proj_kernel,
        grid=(D_CONV // tn, s // tm),
        in_specs=[pl.BlockSpec((tm, d), lambda j, i: (i, 0)),
                  pl.BlockSpec((d, tn), lambda j, i: (0, jb + j)),
                  pl.BlockSpec((d, tn), lambda j, i: (0, jc + j)),
                  pl.BlockSpec((d, tn), lambda j, i: (0, jx + j)),
                  pl.BlockSpec((3, tn), lambda j, i: (0, j))],
        out_specs=pl.BlockSpec((tm, tn), lambda j, i: (i, j)),
        out_shape=jax.ShapeDtypeStruct((s, D_CONV), BF16),
        scratch_shapes=[pltpu.VMEM((8, tn), F32)],
        compiler_params=_params("arbitrary", "arbitrary"),
        name="conv_proj",
    )(h, w_in, w_in, w_in, conv_w)


TILE_BLOCKS = 2
ATTN_TILE = TILE_BLOCKS * MOBA_BLOCK
BF16_SUBLANES = 16
ONES_ROWS = BF16_SUBLANES
SCORES_AHEAD = 2


def _attn_kernel(qlo_ref, qhi_ref, k_ref, vt_ref, olo_ref, ohi_ref,
                 kaug_ref, qaug_ref, kmean_ref, acc_ref, m_ref, *, n_blocks):
    hd = pl.program_id(0)
    step = pl.program_id(1)
    blk, d, tile = MOBA_BLOCK, HEAD_DIM, ATTN_TILE
    n_q = n_blocks // TILE_BLOCKS
    gate_rows = kmean_ref.shape[0]
    q_tiles = (step, n_q - 1 - step)

    @pl.when((hd == 0) & (step == 0))
    def _():
        kmean_ref[...] = jnp.zeros_like(kmean_ref)
        lane = lax.broadcasted_iota(jnp.int32, (blk, d), 1)

        def body(b, carry):
            rows = pl.ds(pl.multiple_of(b * blk, blk), blk)
            kaug_ref[rows, d:2 * d] = (lane == b).astype(BF16)
            return carry
        lax.fori_loop(0, n_blocks, body, 0)
        if gate_rows < d:
            for w in range(2):
                qaug_ref[w, d + gate_rows:2 * d, :] = jnp.full(
                    (d - gate_rows, tile), MASK_VALUE, BF16)

    @pl.when(step == 0)
    def _():
        def body(b, carry):
            rows = pl.ds(pl.multiple_of(b * blk, blk), blk)
            kb = k_ref[rows, :]
            kaug_ref[rows, 0:d] = kb
            kmean_ref[pl.ds(b, 1), :] = (
                jnp.sum(kb.astype(F32), axis=0, keepdims=True) * (1.0 / blk))
            return carry
        lax.fori_loop(0, n_blocks, body, 0)

    blk_id = lax.broadcasted_iota(jnp.int32, (gate_rows, tile), 0)
    q_lane = lax.broadcasted_iota(jnp.int32, (gate_rows, tile), 1)
    kmean = kmean_ref[...].astype(BF16)
    for w, (q_ref, q_tile) in enumerate(zip((qlo_ref, qhi_ref), q_tiles)):
        qt = q_ref[...]
        own_blk = q_tile * TILE_BLOCKS + q_lane // blk
        gate = _dot(kmean, qt)
        gate = jnp.where(blk_id < own_blk, gate, -jnp.inf)
        sel = blk_id == own_blk
        for _ in range(MOBA_TOPK):
            mx = jnp.max(gate, axis=0, keepdims=True)
            idx = jnp.min(jnp.where(gate == mx, blk_id, gate_rows), axis=0,
                          keepdims=True)
            hit = blk_id == idx
            sel = sel | (hit & (mx > -jnp.inf))
            gate = jnp.where(hit, -jnp.inf, gate)
        qaug_ref[w, 0:d, :] = qt
        qaug_ref[w, d:d + gate_rows, :] = jnp.where(sel, 0.0, MASK_VALUE).astype(BF16)

    def scores(t, w):
        keys = pl.ds(pl.multiple_of(t * tile, tile), tile)
        return _dot(kaug_ref[keys, :], qaug_ref[w])

    def weighted_values(t, pb):
        out = _dot(vt_ref[0, t * TILE_BLOCKS], pb[0:blk])
        for u in range(1, TILE_BLOCKS):
            out += _dot(vt_ref[0, t * TILE_BLOCKS + u], pb[u * blk:(u + 1) * blk])
        return out

    tasks = [(q_tile, w) for w, q_tile in enumerate(q_tiles)]
    for slot in range(n_q - 1):
        tasks.append((jnp.where(slot < step, slot, slot - step),
                      (slot >= step).astype(jnp.int32)))
    key_pos = lax.broadcasted_iota(jnp.int32, (tile, tile), 0)
    q_pos = lax.broadcasted_iota(jnp.int32, (tile, tile), 1)

    queue = [scores(*tasks[n]) for n in range(SCORES_AHEAD)]
    for n, (t, w) in enumerate(tasks):
        s = queue.pop(0)
        if n + SCORES_AHEAD < len(tasks):
            queue.append(scores(*tasks[n + SCORES_AHEAD]))
        if n < 2:
            s = jnp.where(key_pos <= q_pos, s, MASK_VALUE)
            m_next = jnp.max(s, axis=0, keepdims=True)
        else:
            m_prev = m_ref[w, 0:1, :]
            m_next = jnp.maximum(m_prev, jnp.max(s, axis=0, keepdims=True))
        pv = weighted_values(t, jnp.exp2(s - m_next).astype(BF16))
        m_ref[w] = jnp.broadcast_to(m_next, m_ref.shape[1:])
        if n < 2:
            acc_ref[w] = pv
        else:
            acc_ref[w] = acc_ref[w] * jnp.exp2(m_prev - m_next) + pv

    for w, o_ref in enumerate((olo_ref, ohi_ref)):
        acc = acc_ref[w]
        o_ref[...] = (acc[0:d] / acc[d:d + 1]).T.astype(o_ref.dtype)


def _moba_attention(qk, v):
    s = v.shape[0]
    n_blocks = s // MOBA_BLOCK
    blk, d, tile = MOBA_BLOCK, HEAD_DIM, ATTN_TILE
    n_q = s // tile
    assert s % (2 * tile) == 0 and n_blocks <= LANES
    gate_rows = -(-n_blocks // BF16_SUBLANES) * BF16_SUBLANES
    qt = qk[:, :D_ATTN].T
    vt = v.reshape(n_blocks, blk, N_ATTN_HEADS, d).transpose(2, 0, 3, 1)
    vt = jnp.concatenate(
        [vt, jnp.ones((N_ATTN_HEADS, n_blocks, ONES_ROWS, blk), BF16)], axis=2)
    half = jax.ShapeDtypeStruct((s // 2, D_ATTN), BF16)
    lo, hi = pl.pallas_call(
        functools.partial(_attn_kernel, n_blocks=n_blocks),
        grid=(N_ATTN_HEADS, n_q // 2),
        in_specs=[pl.BlockSpec((d, tile), lambda h, i: (h, i)),
                  pl.BlockSpec((d, tile), lambda h, i: (h, n_q - 1 - i)),
                  pl.BlockSpec((s, d), lambda h, i: (0, N_ATTN_HEADS + h)),
                  pl.BlockSpec((1, n_blocks, d + ONES_ROWS, blk),
                               lambda h, i: (h, 0, 0, 0))],
        out_specs=[pl.BlockSpec((tile, d), lambda h, i: (i, h)),
                   pl.BlockSpec((tile, d), lambda h, i: (n_q // 2 - 1 - i, h))],
        out_shape=[half, half],
        scratch_shapes=[pltpu.VMEM((s, 2 * d), BF16),
                        pltpu.VMEM((2, 2 * d, tile), BF16),
                        pltpu.VMEM((gate_rows, d), F32),
                        pltpu.VMEM((2, d + ONES_ROWS, tile), F32),
                        pltpu.VMEM((2, 8, tile), F32)],
        compiler_params=_params("arbitrary", "arbitrary"),
        name="moba_attention",
    )(qt, qt, qk, vt)
    return jnp.concatenate([lo, hi], axis=0)


def _merge_kernel(a_ref, c_ref, wa_ref, wc_ref, ga_ref, gc_ref, o_ref):
    ya = _dot(a_ref[...], wa_ref[...])
    yc = _dot(c_ref[...], wc_ref[...])
    o_ref[...] = (ga_ref[...].astype(F32) * ya
                  + gc_ref[...].astype(F32) * yc).astype(o_ref.dtype)


def _merge(attn, conv, w_a, w_c, gates):
    s = attn.shape[0]
    n = w_a.shape[1]
    tm, tn = _tile(s, 1024), 1024
    jc = n // tn
    return pl.pallas_call(
        _merge_kernel,
        grid=(n // tn, s // tm),
        in_specs=[pl.BlockSpec((tm, D_ATTN), lambda j, i: (i, 0)),
                  pl.BlockSpec((tm, D_CONV), lambda j, i: (i, 0)),
                  pl.BlockSpec((D_ATTN, tn), lambda j, i: (0, j)),
                  pl.BlockSpec((D_CONV, tn), lambda j, i: (0, j)),
                  pl.BlockSpec((tm, tn), lambda j, i: (i, j)),
                  pl.BlockSpec((tm, tn), lambda j, i: (i, jc + j))],
        out_specs=pl.BlockSpec((tm, tn), lambda j, i: (i, j)),
        out_shape=jax.ShapeDtypeStruct((s, n), BF16),
        compiler_params=_params("arbitrary", "arbitrary"),
        name="merge",
    )(attn, conv, w_a, w_c, gates, gates)


def _residual_proj_kernel(a_ref, w_ref, x_ref, o_ref):
    o_ref[...] = x_ref[...] + _dot(a_ref[...], w_ref[...])


def _residual_proj(a, w, x):
    s, k = a.shape
    n = w.shape[1]
    tm, tn = _tile(s, 1024), 1024
    return pl.pallas_call(
        _residual_proj_kernel,
        grid=(n // tn, s // tm),
        in_specs=[pl.BlockSpec((tm, k), lambda j, i: (i, 0)),
                  pl.BlockSpec((k, tn), lambda j, i: (0, j)),
                  pl.BlockSpec((tm, tn), lambda j, i: (i, j))],
        out_specs=pl.BlockSpec((tm, tn), lambda j, i: (i, j)),
        out_shape=jax.ShapeDtypeStruct((s, n), F32),
        compiler_params=_params("arbitrary", "arbitrary"),
        name="out_proj",
    )(a, w, x)


FFN_UP_COLS = 512
FFN_UP_HALF = FFN_UP_COLS // 2


def _ffn_up_kernel(h_ref, wg_ref, wva_ref, wvb_ref, cwg_ref, cwva_ref, cwvb_ref,
                   o_ref, cg_ref, cv_ref):
    j = pl.program_id(1)

    @pl.when(pl.program_id(0) == 0)
    def _():
        cg_ref[j] = jnp.zeros(cg_ref.shape[1:], F32)
        cv_ref[j] = jnp.zeros(cv_ref.shape[1:], F32)

    h = h_ref[...]
    ug = _causal_conv3(_dot(h, wg_ref[...]), cwg_ref[...], cg_ref.at[j])
    uv = jnp.concatenate([_dot(h, wva_ref[...]), _dot(h, wvb_ref[...])], axis=1)
    cwv = jnp.concatenate([cwva_ref[...], cwvb_ref[...]], axis=1)
    uv = _causal_conv3(uv, cwv, cv_ref.at[j])
    o_ref[...] = (ug * jax.nn.sigmoid(ug) * uv).astype(o_ref.dtype)


def _ffn_up(h, w_up, conv_w):
    s, d = h.shape
    f = w_up.shape[1] // 2
    tm, tn, th = _tile(s, 1024), FFN_UP_COLS, FFN_UP_HALF
    assert f % th == 0
    nj = pl.cdiv(f, tn)
    v0 = f // th
    last = 2 * f // th - 1

    def g_map(i, j):
        return (0, j)

    def va_map(i, j):
        return (0, v0 + 2 * j)

    def vb_map(i, j):
        return (0, jnp.minimum(v0 + 2 * j + 1, last))

    return pl.pallas_call(
        _ffn_up_kernel,
        grid=(s // tm, nj),
        in_specs=[pl.BlockSpec((tm, d), lambda i, j: (i, 0)),
                  pl.BlockSpec((d, tn), g_map),
                  pl.BlockSpec((d, th), va_map),
                  pl.BlockSpec((d, th), vb_map),
                  pl.BlockSpec((3, tn), g_map),
                  pl.BlockSpec((3, th), va_map),
                  pl.BlockSpec((3, th), vb_map)],
        out_specs=pl.BlockSpec((tm, tn), lambda i, j: (i, j)),
        out_shape=jax.ShapeDtypeStruct((s, f), BF16),
        scratch_shapes=[pltpu.VMEM((nj, 8, tn), F32),
                        pltpu.VMEM((nj, 8, tn), F32)],
        compiler_params=_params("arbitrary", "arbitrary"),
        name="ffn_up",
    )(h, w_up, w_up, w_up, conv_w, conv_w, conv_w)


def _ffn_down_kernel(a_ref, w_ref, x_ref, o_ref, ob_ref):
    out = x_ref[...] + _dot(a_ref[...], w_ref[...])
    o_ref[...] = out
    ob_ref[...] = out.astype(ob_ref.dtype)


def _ffn_down(act, w, x):
    s, f = act.shape
    n = w.shape[1]
    tm, tn = _tile(s, 512), 512
    out = pl.BlockSpec((tm, tn), lambda j, i: (i, j))
    return pl.pallas_call(
        _ffn_down_kernel,
        grid=(n // tn, s // tm),
        in_specs=[pl.BlockSpec((tm, f), lambda j, i: (i, 0)),
                  pl.BlockSpec((f, tn), lambda j, i: (0, j)),
                  out],
        out_specs=[out, out],
        out_shape=[jax.ShapeDtypeStruct((s, n), F32),
                   jax.ShapeDtypeStruct((s, n), BF16)],
        compiler_params=_params("arbitrary", "arbitrary"),
        name="ffn_down",
    )(act, w, x)


def _ple_kernel(xb_ref, wpg_ref, p_ref, wp_ref, x_ref, o_ref):
    wpg, wp = wpg_ref[...], wp_ref[...]
    chunks = _row_chunks(xb_ref)
    raw = [(_dot(xb_ref[rows, :], wpg), _dot(p_ref[rows, :].astype(BF16), wp))
           for rows in chunks]
    for rows, (gate, emb) in zip(chunks, raw):
        o_ref[rows, :] = x_ref[rows, :] + jax.nn.sigmoid(gate) * emb


def _ple(xb, w_pg, p, w_p, x):
    s, d = xb.shape
    n = w_pg.shape[1]
    dp = p.shape[1]
    tm, tn = _tile(s, 1024), 512
    return pl.pallas_call(
        _ple_kernel,
        grid=(n // tn, s // tm),
        in_specs=[pl.BlockSpec((tm, d), lambda j, i: (i, 0)),
                  pl.BlockSpec((d, tn), lambda j, i: (0, j)),
                  pl.BlockSpec((tm, dp), lambda j, i: (i, 0)),
                  pl.BlockSpec((dp, tn), lambda j, i: (0, j)),
                  pl.BlockSpec((tm, tn), lambda j, i: (i, j))],
        out_specs=pl.BlockSpec((tm, tn), lambda j, i: (i, j)),
        out_shape=jax.ShapeDtypeStruct((s, n), F32),
        compiler_params=_params("arbitrary", "arbitrary"),
        name="ple",
    )(xb, w_pg, p, w_p, x)


def _layer(x, p, cos, sin, w_norm_mix, w_in, q_norm, k_norm, conv_mix_w,
           w_attn_branch, w_conv_branch, w_out, w_norm_ffn, w_up, ffn_conv_w,
           w_down, w_ple, w_ple_gate):
    w_in_b = w_in.astype(BF16)

    gains = jnp.concatenate([jnp.tile(q_norm * (HEAD_DIM ** -0.5 * LOG2_E), N_ATTN_HEADS),
                             jnp.tile(k_norm, N_ATTN_HEADS)]).reshape(1, 2 * D_ATTN)

    h = _rmsnorm(x, w_norm_mix)
    qk = _qk_proj(h, w_in_b, gains, cos, sin)
    v = _plain_proj(h, w_in_b, 2 * D_ATTN, D_ATTN, sigmoid=False)
    conv = _conv_proj(h, w_in_b, conv_mix_w, 3 * D_ATTN)
    gates = _plain_proj(h, w_in_b, 3 * D_ATTN + 3 * D_CONV, 2 * x.shape[1],
                        sigmoid=True)
    attn = _moba_attention(qk, v)
    merged = _merge(attn, conv, w_attn_branch.astype(BF16),
                    w_conv_branch.astype(BF16), gates)
    x1 = _residual_proj(merged, w_out.astype(BF16), x)

    h2 = _rmsnorm(x1, w_norm_ffn)
    act = _ffn_up(h2, w_up.astype(BF16), ffn_conv_w)
    x2, x2b = _ffn_down(act, w_down.astype(BF16), x1)

    return _ple(x2b, w_ple_gate.astype(BF16), p, w_ple.astype(BF16), x2)


def kernel(x, p, positions, w_norm_mix, w_in, q_norm, k_norm, conv_mix_w,
           w_attn_branch, w_conv_branch, w_out, w_norm_ffn, w_up, ffn_conv_w,
           w_down, w_ple, w_ple_gate):
    b, s, d = x.shape
    assert b == 1, "single-sequence prefill only"
    xs = x.reshape(s, d)
    cos, sin = _rope_tables(positions.reshape(s))
    for i in range(w_in.shape[0]):
        xs = _layer(xs, p[i].reshape(s, -1), cos, sin, w_norm_mix[i], w_in[i],
                    q_norm[i], k_norm[i], conv_mix_w[i], w_attn_branch[i],
                    w_conv_branch[i], w_out[i], w_norm_ffn[i], w_up[i],
                    ffn_conv_w[i], w_down[i], w_ple[i], w_ple_gate[i])
    return xs.reshape(b, s, d)
```

```python
import functools
import math
from typing import NamedTuple

import jax
import jax.numpy as jnp
from jax import lax
from jax.experimental import pallas as pl
from jax.experimental.pallas import tpu as pltpu

N_ATTN_HEADS = 16
HEAD_DIM = 128
MOBA_BLOCK = 256
MOBA_TOPK = 3
ROPE_THETA = 10000.0
RMS_EPS = 1e-6
D_ATTN = N_ATTN_HEADS * HEAD_DIM
D_CONV = 2048

LANES = 128
MASK_VALUE = -1e30
LOG2_E = math.log2(math.e)
EPILOGUE_CHUNKS = 4
VMEM_LIMIT_BYTES = 56 * 1024 * 1024
BF16 = jnp.bfloat16
F32 = jnp.float32


def _params(*semantics):
    return pltpu.CompilerParams(dimension_semantics=semantics,
                                vmem_limit_bytes=VMEM_LIMIT_BYTES)


def _tile(n, preferred):
    return preferred if n % preferred == 0 else n


def _dot(a, b):
    return jnp.dot(a, b, preferred_element_type=F32)


def _dot_nt(a, b):
    return lax.dot_general(a, b, (((1,), (1,)), ((), ())),
                           preferred_element_type=F32)


class SideCast(NamedTuple):
    w: jax.Array
    col0: int
    width: int
    block_cols: int


def _whole(w):
    return SideCast(w, 0, w.shape[1], w.shape[1])


def _hosted_call(body, *, grid, in_specs, out_specs, out_shape, args, name,
                 scratch_shapes=(), side=()):
    out_specs = list(out_specs) if isinstance(out_specs, (list, tuple)) else [out_specs]
    out_shape = list(out_shape) if isinstance(out_shape, (list, tuple)) else [out_shape]
    n_in, n_out, n_side = len(in_specs), len(out_specs), len(side)
    n_steps = math.prod(grid)

    def step_of(*ids):
        t = ids[0]
        for extent, i in zip(grid[1:], ids[1:]):
            t = t * extent + i
        return t

    side_in, side_out, side_shape = [], [], []
    for cast in side:
        rows = cast.w.shape[0]
        assert cast.width % cast.block_cols == 0 and cast.col0 % cast.block_cols == 0
        ncb = cast.width // cast.block_cols
        cb0 = cast.col0 // cast.block_cols
        rb = pl.cdiv(pl.cdiv(rows, n_steps // ncb), BF16_SUBLANES) * BF16_SUBLANES
        last = pl.cdiv(rows, rb) * ncb - 1

        def in_map(*ids, ncb=ncb, cb0=cb0, last=last):
            t = jnp.minimum(step_of(*ids), last)
            return (t // ncb, cb0 + t % ncb)

        def out_map(*ids, ncb=ncb, last=last):
            t = jnp.minimum(step_of(*ids), last)
            return (t // ncb, t % ncb)

        side_in.append(pl.BlockSpec((rb, cast.block_cols), in_map))
        side_out.append(pl.BlockSpec((rb, cast.block_cols), out_map))
        side_shape.append(jax.ShapeDtypeStruct((rows, cast.width), BF16))

    def kernel(*refs):
        ins = refs[:n_in]
        cast_src = refs[n_in:n_in + n_side]
        outs = refs[n_in + n_side:n_in + n_side + n_out]
        cast_dst = refs[n_in + n_side + n_out:n_in + 2 * n_side + n_out]
        scratch = refs[n_in + 2 * n_side + n_out:]
        for src_ref, dst_ref in zip(cast_src, cast_dst):
            dst_ref[...] = src_ref[...].astype(dst_ref.dtype)
        body(*ins, *outs, *scratch)

    return pl.pallas_call(
        kernel,
        grid=grid,
        in_specs=list(in_specs) + side_in,
        out_specs=out_specs + side_out,
        out_shape=out_shape + side_shape,
        scratch_shapes=list(scratch_shapes),
        compiler_params=_params(*(("arbitrary",) * len(grid))),
        name=name,
    )(*args, *[cast.w for cast in side])


def _rmsnorm_kernel(x_ref, g_ref, o_ref):
    x = x_ref[...]
    ms = jnp.mean(x * x, axis=-1, keepdims=True)
    o_ref[...] = (x * lax.rsqrt(ms + RMS_EPS) * g_ref[...]).astype(o_ref.dtype)


def _rmsnorm(x, g):
    s, d = x.shape
    tm = _tile(s, 256)
    return pl.pallas_call(
        _rmsnorm_kernel,
        grid=(s // tm,),
        in_specs=[pl.BlockSpec((tm, d), lambda i: (i, 0)),
                  pl.BlockSpec((1, d), lambda i: (0, 0))],
        out_specs=pl.BlockSpec((tm, d), lambda i: (i, 0)),
        out_shape=jax.ShapeDtypeStruct((s, d), BF16),
        compiler_params=_params("arbitrary"),
        name="rmsnorm",
    )(x, g.reshape(1, d))


def _rope_table_kernel(pos_ref, invf_ref, sign_ref, cos_ref, sin_ref):
    ang = pos_ref[...].astype(F32) * invf_ref[...]
    cos_ref[...] = jnp.cos(ang)
    sin_ref[...] = jnp.sin(ang) * sign_ref[...]


def _rope_tables(positions):
    s = positions.shape[0]
    half = HEAD_DIM // 2
    inv_freq = ROPE_THETA ** (-jnp.arange(half, dtype=F32) / half)
    invf = jnp.concatenate([inv_freq, inv_freq]).reshape(1, HEAD_DIM)
    sign = jnp.concatenate([-jnp.ones((half,), F32),
                            jnp.ones((half,), F32)]).reshape(1, HEAD_DIM)
    tm = _tile(s, 1024)
    row = pl.BlockSpec((1, HEAD_DIM), lambda i: (0, 0))
    tab = pl.BlockSpec((tm, HEAD_DIM), lambda i: (i, 0))
    return pl.pallas_call(
        _rope_table_kernel,
        grid=(s // tm,),
        in_specs=[pl.BlockSpec((tm, 1), lambda i: (i, 0)), row, row],
        out_specs=[tab, tab],
        out_shape=[jax.ShapeDtypeStruct((s, HEAD_DIM), F32)] * 2,
        compiler_params=_params("arbitrary"),
        name="rope_tables",
    )(positions.reshape(s, 1), invf, sign)


def _qk_proj_kernel(h_ref, w_ref, g_ref, cos_ref, sin_ref, o_ref):
    tm = h_ref.shape[0]
    chunk = tm // EPILOGUE_CHUNKS
    w = w_ref[...]
    accs = [_dot(h_ref[c * chunk:(c + 1) * chunk, :], w)
            for c in range(EPILOGUE_CHUNKS)]
    for c, acc in enumerate(accs):
        rows = slice(c * chunk, (c + 1) * chunk)
        cos = cos_ref[rows, :]
        sin = sin_ref[rows, :]
        for hd in range(acc.shape[1] // HEAD_DIM):
            cols = slice(hd * HEAD_DIM, (hd + 1) * HEAD_DIM)
            a = acc[:, cols]
            ms = jnp.mean(a * a, axis=-1, keepdims=True)
            y = a * lax.rsqrt(ms + RMS_EPS) * g_ref[:, cols]
            r = y * cos + pltpu.roll(y, HEAD_DIM // 2, axis=1) * sin
            o_ref[rows, cols] = r.astype(o_ref.dtype)


def _row_chunks(ref):
    chunk = ref.shape[0] // EPILOGUE_CHUNKS
    return [slice(c * chunk, (c + 1) * chunk) for c in range(EPILOGUE_CHUNKS)]


def _plain_proj_kernel(h_ref, w_ref, o_ref, *, sigmoid):
    w = w_ref[...]
    chunks = _row_chunks(h_ref)
    accs = [_dot(h_ref[rows, :], w) for rows in chunks]
    for rows, acc in zip(chunks, accs):
        if sigmoid:
            acc = jax.nn.sigmoid(acc)
        o_ref[rows, :] = acc.astype(o_ref.dtype)


def _qk_proj(h, w_qk, gains, cos, sin, side=()):
    s, d = h.shape
    n = 2 * D_ATTN
    tm, tn = _tile(s, 1024), 1024
    tab = pl.BlockSpec((tm, HEAD_DIM), lambda j, i: (i, 0))
    return _hosted_call(
        _qk_proj_kernel,
        grid=(n // tn, s // tm),
        in_specs=[pl.BlockSpec((tm, d), lambda j, i: (i, 0)),
                  pl.BlockSpec((d, tn), lambda j, i: (0, j)),
                  pl.BlockSpec((1, tn), lambda j, i: (0, j)),
                  tab, tab],
        out_specs=pl.BlockSpec((tm, tn), lambda j, i: (i, j)),
        out_shape=jax.ShapeDtypeStruct((s, n), BF16),
        args=(h, w_qk, gains, cos, sin), name="qk_proj", side=side)


def _plain_proj(h, w, *, sigmoid, side=()):
    s, d = h.shape
    n = w.shape[1]
    tm, tn = _tile(s, 1024), 1024
    return _hosted_call(
        functools.partial(_plain_proj_kernel, sigmoid=sigmoid),
        grid=(n // tn, s // tm),
        in_specs=[pl.BlockSpec((tm, d), lambda j, i: (i, 0)),
                  pl.BlockSpec((d, tn), lambda j, i: (0, j))],
        out_specs=pl.BlockSpec((tm, tn), lambda j, i: (i, j)),
        out_shape=jax.ShapeDtypeStruct((s, n), BF16),
        args=(h, w), name="gate_proj" if sigmoid else "v_proj", side=side)


def _causal_conv3(m, w, carry_ref):
    tm = m.shape[0]
    rows = lax.broadcasted_iota(jnp.int32, m.shape, 0)
    prev = carry_ref[...]
    c2 = prev[6:7]
    c1 = prev[7:8]
    m1 = jnp.where(rows == 0, c1, pltpu.roll(m, 1, axis=0))
    m2 = jnp.where(rows == 0, c2,
                   jnp.where(rows == 1, c1, pltpu.roll(m, 2, axis=0)))
    carry_ref[...] = m[tm - 8:tm]
    return w[2:3] * m + w[0:1] * m2 + w[1:2] * m1


def _conv_proj_kernel(h_ref, wb_ref, wc_ref, wx_ref, cw_ref, o_ref, carry_ref):
    @pl.when(pl.program_id(1) == 0)
    def _():
        carry_ref[...] = jnp.zeros_like(carry_ref)

    h = h_ref[...]
    cb = _dot(h, wb_ref[...])
    m = _dot(h, wc_ref[...]) * _dot(h, wx_ref[...])
    y = _causal_conv3(m, cw_ref[...], carry_ref)
    o_ref[...] = (cb * y).astype(o_ref.dtype)


def _conv_proj(h, w, conv_w, side=()):
    s, d = h.shape
    tm, tn = _tile(s, 1024), 256
    jc, jx = D_CONV // tn, 2 * D_CONV // tn
    return _hosted_call(
        _conv_proj_kernel,
        grid=(D_CONV // tn, s // tm),
        in_specs=[pl.BlockSpec((tm, d), lambda j, i: (i, 0)),
                  pl.BlockSpec((d, tn), lambda j, i: (0, j)),
                  pl.BlockSpec((d, tn), lambda j, i: (0, jc + j)),
                  pl.BlockSpec((d, tn), lambda j, i: (0, jx + j)),
                  pl.BlockSpec((3, tn), lambda j, i: (0, j))],
        out_specs=pl.BlockSpec((tm, tn), lambda j, i: (i, j)),
        out_shape=jax.ShapeDtypeStruct((s, D_CONV), BF16),
        scratch_shapes=[pltpu.VMEM((8, tn), F32)],
        args=(h, w, w, w, conv_w), name="conv_proj", side=side)


TILE_BLOCKS = 2
ATTN_TILE = TILE_BLOCKS * MOBA_BLOCK
BF16_SUBLANES = 16
ONES_ROWS = BF16_SUBLANES
SCORES_AHEAD = 2


def _attn_kernel(qlo_ref, qhi_ref, k_ref, vt_ref, olo_ref, ohi_ref,
                 kaug_ref, qaug_ref, kmean_ref, acc_ref, m_ref, *, n_blocks):
    hd = pl.program_id(0)
    step = pl.program_id(1)
    blk, d, tile = MOBA_BLOCK, HEAD_DIM, ATTN_TILE
    n_q = n_blocks // TILE_BLOCKS
    gate_rows = kmean_ref.shape[0]
    q_tiles = (step, n_q - 1 - step)

    @pl.when((hd == 0) & (step == 0))
    def _():
        kmean_ref[...] = jnp.zeros_like(kmean_ref)
        lane = lax.broadcasted_iota(jnp.int32, (blk, d), 1)

        def body(b, carry):
            rows = pl.ds(pl.multiple_of(b * blk, blk), blk)
            kaug_ref[rows, d:2 * d] = (lane == b).astype(BF16)
            return carry
        lax.fori_loop(0, n_blocks, body, 0)
        if gate_rows < d:
            for w in range(2):
                qaug_ref[w, d + gate_rows:2 * d, :] = jnp.full(
                    (d - gate_rows, tile), MASK_VALUE, BF16)

    @pl.when(step == 0)
    def _():
        def body(b, carry):
            rows = pl.ds(pl.multiple_of(b * blk, blk), blk)
            kb = k_ref[rows, :]
            kaug_ref[rows, 0:d] = kb
            kmean_ref[pl.ds(b, 1), :] = (
                jnp.sum(kb.astype(F32), axis=0, keepdims=True) * (1.0 / blk))
            return carry
        lax.fori_loop(0, n_blocks, body, 0)

    blk_id = lax.broadcasted_iota(jnp.int32, (gate_rows, tile), 0)
    q_lane = lax.broadcasted_iota(jnp.int32, (gate_rows, tile), 1)
    kmean = kmean_ref[...].astype(BF16)
    for w, (q_ref, q_tile) in enumerate(zip((qlo_ref, qhi_ref), q_tiles)):
        qt = q_ref[...]
        own_blk = q_tile * TILE_BLOCKS + q_lane // blk
        gate = _dot(kmean, qt)
        gate = jnp.where(blk_id < own_blk, gate, -jnp.inf)
        sel = blk_id == own_blk
        for _ in range(MOBA_TOPK):
            mx = jnp.max(gate, axis=0, keepdims=True)
            idx = jnp.min(jnp.where(gate == mx, blk_id, gate_rows), axis=0,
                          keepdims=True)
            hit = blk_id == idx
            sel = sel | (hit & (mx > -jnp.inf))
            gate = jnp.where(hit, -jnp.inf, gate)
        qaug_ref[w, 0:d, :] = qt
        qaug_ref[w, d:d + gate_rows, :] = jnp.where(sel, 0.0, MASK_VALUE).astype(BF16)

    def scores(t, w):
        keys = pl.ds(pl.multiple_of(t * tile, tile), tile)
        return _dot(kaug_ref[keys, :], qaug_ref[w])

    def weighted_values(t, pb):
        out = _dot(vt_ref[0, t * TILE_BLOCKS], pb[0:blk])
        for u in range(1, TILE_BLOCKS):
            out += _dot(vt_ref[0, t * TILE_BLOCKS + u], pb[u * blk:(u + 1) * blk])
        return out

    tasks = [(q_tile, w) for w, q_tile in enumerate(q_tiles)]
    for slot in range(n_q - 1):
        tasks.append((jnp.where(slot < step, slot, slot - step),
                      (slot >= step).astype(jnp.int32)))
    key_pos = lax.broadcasted_iota(jnp.int32, (tile, tile), 0)
    q_pos = lax.broadcasted_iota(jnp.int32, (tile, tile), 1)

    queue = [scores(*tasks[n]) for n in range(SCORES_AHEAD)]
    for n, (t, w) in enumerate(tasks):
        s = queue.pop(0)
        if n + SCORES_AHEAD < len(tasks):
            queue.append(scores(*tasks[n + SCORES_AHEAD]))
        if n < 2:
            s = jnp.where(key_pos <= q_pos, s, MASK_VALUE)
            m_next = jnp.max(s, axis=0, keepdims=True)
        else:
            m_prev = m_ref[w, 0:1, :]
            m_next = jnp.maximum(m_prev, jnp.max(s, axis=0, keepdims=True))
        pv = weighted_values(t, jnp.exp2(s - m_next).astype(BF16))
        m_ref[w] = jnp.broadcast_to(m_next, m_ref.shape[1:])
        if n < 2:
            acc_ref[w] = pv
        else:
            acc_ref[w] = acc_ref[w] * jnp.exp2(m_prev - m_next) + pv

    for w, o_ref in enumerate((olo_ref, ohi_ref)):
        acc = acc_ref[w]
        o_ref[...] = (acc[0:d] / acc[d:d + 1]).T.astype(o_ref.dtype)


def _moba_attention(qk, v):
    s = v.shape[0]
    n_blocks = s // MOBA_BLOCK
    blk, d, tile = MOBA_BLOCK, HEAD_DIM, ATTN_TILE
    n_q = s // tile
    assert s % (2 * tile) == 0 and n_blocks <= LANES
    gate_rows = -(-n_blocks // BF16_SUBLANES) * BF16_SUBLANES
    qt = qk[:, :D_ATTN].T
    vt = v.reshape(n_blocks, blk, N_ATTN_HEADS, d).transpose(2, 0, 3, 1)
    vt = jnp.concatenate(
        [vt, jnp.ones((N_ATTN_HEADS, n_blocks, ONES_ROWS, blk), BF16)], axis=2)
    half = jax.ShapeDtypeStruct((s // 2, D_ATTN), BF16)
    lo, hi = pl.pallas_call(
        functools.partial(_attn_kernel, n_blocks=n_blocks),
        grid=(N_ATTN_HEADS, n_q // 2),
        in_specs=[pl.BlockSpec((d, tile), lambda h, i: (h, i)),
                  pl.BlockSpec((d, tile), lambda h, i: (h, n_q - 1 - i)),
                  pl.BlockSpec((s, d), lambda h, i: (0, N_ATTN_HEADS + h)),
                  pl.BlockSpec((1, n_blocks, d + ONES_ROWS, blk),
                               lambda h, i: (h, 0, 0, 0))],
        out_specs=[pl.BlockSpec((tile, d), lambda h, i: (i, h)),
                   pl.BlockSpec((tile, d), lambda h, i: (n_q // 2 - 1 - i, h))],
        out_shape=[half, half],
        scratch_shapes=[pltpu.VMEM((s, 2 * d), BF16),
                        pltpu.VMEM((2, 2 * d, tile), BF16),
                        pltpu.VMEM((gate_rows, d), F32),
                        pltpu.VMEM((2, d + ONES_ROWS, tile), F32),
                        pltpu.VMEM((2, 8, tile), F32)],
        compiler_params=_params("arbitrary", "arbitrary"),
        name="moba_attention",
    )(qt, qt, qk, vt)
    return jnp.concatenate([lo, hi], axis=0)


def _merge_kernel(a_ref, c_ref, wa_ref, wc_ref, ga_ref, gc_ref, o_ref):
    ya = _dot(a_ref[...], wa_ref[...])
    yc = _dot(c_ref[...], wc_ref[...])
    o_ref[...] = (ga_ref[...].astype(F32) * ya
                  + gc_ref[...].astype(F32) * yc).astype(o_ref.dtype)


def _merge(attn, conv, w_a, w_c, gates, side=()):
    s = attn.shape[0]
    n = w_a.shape[1]
    tm, tn = _tile(s, 1024), 1024
    jc = n // tn
    return _hosted_call(
        _merge_kernel,
        grid=(n // tn, s // tm),
        in_specs=[pl.BlockSpec((tm, D_ATTN), lambda j, i: (i, 0)),
                  pl.BlockSpec((tm, D_CONV), lambda j, i: (i, 0)),
                  pl.BlockSpec((D_ATTN, tn), lambda j, i: (0, j)),
                  pl.BlockSpec((D_CONV, tn), lambda j, i: (0, j)),
                  pl.BlockSpec((tm, tn), lambda j, i: (i, j)),
                  pl.BlockSpec((tm, tn), lambda j, i: (i, jc + j))],
        out_specs=pl.BlockSpec((tm, tn), lambda j, i: (i, j)),
        out_shape=jax.ShapeDtypeStruct((s, n), BF16),
        args=(attn, conv, w_a, w_c, gates, gates), name="merge", side=side)


def _residual_proj_kernel(a_ref, w_ref, x_ref, o_ref):
    o_ref[...] = x_ref[...] + _dot(a_ref[...], w_ref[...])


def _residual_proj(a, w, x):
    s, k = a.shape
    n = w.shape[1]
    tm, tn = _tile(s, 1024), 1024
    return pl.pallas_call(
        _residual_proj_kernel,
        grid=(n // tn, s // tm),
        in_specs=[pl.BlockSpec((tm, k), lambda j, i: (i, 0)),
                  pl.BlockSpec((k, tn), lambda j, i: (0, j)),
                  pl.BlockSpec((tm, tn), lambda j, i: (i, j))],
        out_specs=pl.BlockSpec((tm, tn), lambda j, i: (i, j)),
        out_shape=jax.ShapeDtypeStruct((s, n), F32),
        compiler_params=_params("arbitrary", "arbitrary"),
        name="out_proj",
    )(a, w, x)


FFN_UP_COLS = 512
FFN_UP_HALF = FFN_UP_COLS // 2


def _ffn_up_kernel(h_ref, wg_ref, wva_ref, wvb_ref, cwg_ref, cwva_ref, cwvb_ref,
                   o_ref, cg_ref, cv_ref):
    j = pl.program_id(1)

    @pl.when(pl.program_id(0) == 0)
    def _():
        cg_ref[j] = jnp.zeros(cg_ref.shape[1:], F32)
        cv_ref[j] = jnp.zeros(cv_ref.shape[1:], F32)

    h = h_ref[...]
    ug = _causal_conv3(_dot(h, wg_ref[...]), cwg_ref[...], cg_ref.at[j])
    uv = jnp.concatenate([_dot(h, wva_ref[...]), _dot(h, wvb_ref[...])], axis=1)
    cwv = jnp.concatenate([cwva_ref[...], cwvb_ref[...]], axis=1)
    uv = _causal_conv3(uv, cwv, cv_ref.at[j])
    o_ref[...] = (ug * jax.nn.sigmoid(ug) * uv).astype(o_ref.dtype)


def _ffn_up(h, w_up, conv_w, side=()):
    s, d = h.shape
    f = w_up.shape[1] // 2
    tm, tn, th = _tile(s, 1024), FFN_UP_COLS, FFN_UP_HALF
    assert f % th == 0
    nj = pl.cdiv(f, tn)
    v0 = f // th
    last = 2 * f // th - 1

    def g_map(i, j):
        return (0, j)

    def va_map(i, j):
        return (0, v0 + 2 * j)

    def vb_map(i, j):
        return (0, jnp.minimum(v0 + 2 * j + 1, last))

    return _hosted_call(
        _ffn_up_kernel,
        grid=(s // tm, nj),
        in_specs=[pl.BlockSpec((tm, d), lambda i, j: (i, 0)),
                  pl.BlockSpec((d, tn), g_map),
                  pl.BlockSpec((d, th), va_map),
                  pl.BlockSpec((d, th), vb_map),
                  pl.BlockSpec((3, tn), g_map),
                  pl.BlockSpec((3, th), va_map),
                  pl.BlockSpec((3, th), vb_map)],
        out_specs=pl.BlockSpec((tm, tn), lambda i, j: (i, j)),
        out_shape=jax.ShapeDtypeStruct((s, f), BF16),
        scratch_shapes=[pltpu.VMEM((nj, 8, tn), F32),
                        pltpu.VMEM((nj, 8, tn), F32)],
        args=(h, w_up, w_up, w_up, conv_w, conv_w, conv_w), name="ffn_up", side=side)


def _ffn_down_kernel(a_ref, w_ref, x_ref, o_ref, ob_ref):
    out = x_ref[...] + _dot(a_ref[...], w_ref[...])
    o_ref[...] = out
    ob_ref[...] = out.astype(ob_ref.dtype)


def _ffn_down(act, w, x):
    s, f = act.shape
    n = w.shape[1]
    tm, tn = _tile(s, 512), 512
    out = pl.BlockSpec((tm, tn), lambda j, i: (i, j))
    return pl.pallas_call(
        _ffn_down_kernel,
        grid=(n // tn, s // tm),
        in_specs=[pl.BlockSpec((tm, f), lambda j, i: (i, 0)),
                  pl.BlockSpec((f, tn), lambda j, i: (0, j)),
                  out],
        out_specs=[out, out],
        out_shape=[jax.ShapeDtypeStruct((s, n), F32),
                   jax.ShapeDtypeStruct((s, n), BF16)],
        compiler_params=_params("arbitrary", "arbitrary"),
        name="ffn_down",
    )(act, w, x)


def _ple_kernel(xb_ref, wpg_ref, p_ref, wp_ref, x_ref, o_ref):
    wpg, wp = wpg_ref[...], wp_ref[...]
    chunks = _row_chunks(xb_ref)
    raw = [(_dot(xb_ref[rows, :], wpg), _dot(p_ref[rows, :].astype(BF16), wp))
           for rows in chunks]
    for rows, (gate, emb) in zip(chunks, raw):
        o_ref[rows, :] = x_ref[rows, :] + jax.nn.sigmoid(gate) * emb


def _ple(xb, w_pg, p, w_p, x):
    s, d = xb.shape
    n = w_pg.shape[1]
    dp = p.shape[1]
    tm, tn = _tile(s, 1024), 512
    return pl.pallas_call(
        _ple_kernel,
        grid=(n // tn, s // tm),
        in_specs=[pl.BlockSpec((tm, d), lambda j, i: (i, 0)),
                  pl.BlockSpec((d, tn), lambda j, i: (0, j)),
                  pl.BlockSpec((tm, dp), lambda j, i: (i, 0)),
                  pl.BlockSpec((dp, tn), lambda j, i: (0, j)),
                  pl.BlockSpec((tm, tn), lambda j, i: (i, j))],
        out_specs=pl.BlockSpec((tm, tn), lambda j, i: (i, j)),
        out_shape=jax.ShapeDtypeStruct((s, n), F32),
        compiler_params=_params("arbitrary", "arbitrary"),
        name="ple",
    )(xb, w_pg, p, w_p, x)


def _layer(x, p, cos, sin, w_norm_mix, w_in, q_norm, k_norm, conv_mix_w,
           w_attn_branch, w_conv_branch, w_out, w_norm_ffn, w_up, ffn_conv_w,
           w_down, w_ple, w_ple_gate):
    d_model = x.shape[1]
    v0, c0, g0 = 2 * D_ATTN, 3 * D_ATTN, 3 * D_ATTN + 3 * D_CONV

    gains = jnp.concatenate([jnp.tile(q_norm * (HEAD_DIM ** -0.5 * LOG2_E), N_ATTN_HEADS),
                             jnp.tile(k_norm, N_ATTN_HEADS)]).reshape(1, 2 * D_ATTN)

    h = _rmsnorm(x, w_norm_mix)
    qk, w_v, w_conv = _qk_proj(
        h, w_in[:, :v0].astype(BF16), gains, cos, sin,
        side=(SideCast(w_in, v0, D_ATTN, D_ATTN),
              SideCast(w_in, c0, 3 * D_CONV, 3 * D_CONV)))
    v, w_a, w_c = _plain_proj(h, w_v, sigmoid=False,
                              side=(_whole(w_attn_branch), _whole(w_conv_branch)))
    conv, w_gate, w_out_b = _conv_proj(
        h, w_conv, conv_mix_w,
        side=(SideCast(w_in, g0, 2 * d_model, d_model), _whole(w_out)))
    gates, w_up_b = _plain_proj(h, w_gate, sigmoid=True, side=(_whole(w_up),))
    attn = _moba_attention(qk, v)
    merged, w_pg = _merge(attn, conv, w_a, w_c, gates, side=(_whole(w_ple_gate),))
    x1 = _residual_proj(merged, w_out_b, x)

    h2 = _rmsnorm(x1, w_norm_ffn)
    act, w_down_b = _ffn_up(h2, w_up_b, ffn_conv_w, side=(_whole(w_down),))
    x2, x2b = _ffn_down(act, w_down_b, x1)

    return _ple(x2b, w_pg, p, w_ple.astype(BF16), x2)


def kernel(x, p, positions, w_norm_mix, w_in, q_norm, k_norm, conv_mix_w,
           w_attn_branch, w_conv_branch, w_out, w_norm_ffn, w_up, ffn_conv_w,
           w_down, w_ple, w_ple_gate):
    b, s, d = x.shape
    assert b == 1, "single-sequence prefill only"
    xs = x.reshape(s, d)
    cos, sin = _rope_tables(positions.reshape(s))
    for i in range(w_in.shape[0]):
        xs = _layer(xs, p[i].reshape(s, -1), cos, sin, w_norm_mix[i], w_in[i],
                    q_norm[i], k_norm[i], conv_mix_w[i], w_attn_branch[i],
                    w_conv_branch[i], w_out[i], w_norm_ffn[i], w_up[i],
                    ffn_conv_w[i], w_down[i], w_ple[i], w_ple_gate[i])
    return xs.reshape(b, s, d)
```

```python
import functools
import math
from typing import NamedTuple

import jax
import jax.numpy as jnp
from jax import lax
from jax.experimental import pallas as pl
from jax.experimental.pallas import tpu as pltpu

N_ATTN_HEADS = 16
HEAD_DIM = 128
MOBA_BLOCK = 256
MOBA_TOPK = 3
ROPE_THETA = 10000.0
RMS_EPS = 1e-6
D_ATTN = N_ATTN_HEADS * HEAD_DIM
D_CONV = 2048

LANES = 128
MASK_VALUE = -1e30
LOG2_E = math.log2(math.e)
EPILOGUE_CHUNKS = 4
VMEM_LIMIT_BYTES = 60 * 1024 * 1024
BF16 = jnp.bfloat16
F32 = jnp.float32


def _params(*semantics):
    return pltpu.CompilerParams(dimension_semantics=semantics,
                                vmem_limit_bytes=VMEM_LIMIT_BYTES)


def _tile(n, preferred):
    return preferred if n % preferred == 0 else n


def _dot(a, b):
    return jnp.dot(a, b, preferred_element_type=F32)


def _dot_nt(a, b):
    return lax.dot_general(a, b, (((1,), (1,)), ((), ())),
                           preferred_element_type=F32)


class SideCast(NamedTuple):
    w: jax.Array
    col0: int
    width: int
    block_cols: int


def _whole(w):
    return SideCast(w, 0, w.shape[1], w.shape[1])


def _hosted_call(body, *, grid, in_specs, out_specs, out_shape, args, name,
                 scratch_shapes=(), side=()):
    out_specs = list(out_specs) if isinstance(out_specs, (list, tuple)) else [out_specs]
    out_shape = list(out_shape) if isinstance(out_shape, (list, tuple)) else [out_shape]
    n_in, n_out, n_side = len(in_specs), len(out_specs), len(side)
    n_steps = math.prod(grid)

    def step_of(*ids):
        t = ids[0]
        for extent, i in zip(grid[1:], ids[1:]):
            t = t * extent + i
        return t

    side_in, side_out, side_shape = [], [], []
    for cast in side:
        rows = cast.w.shape[0]
        assert cast.width % cast.block_cols == 0 and cast.col0 % cast.block_cols == 0
        ncb = cast.width // cast.block_cols
        cb0 = cast.col0 // cast.block_cols
        rb = pl.cdiv(pl.cdiv(rows, n_steps // ncb), BF16_SUBLANES) * BF16_SUBLANES
        last = pl.cdiv(rows, rb) * ncb - 1

        def in_map(*ids, ncb=ncb, cb0=cb0, last=last):
            t = jnp.minimum(step_of(*ids), last)
            return (t // ncb, cb0 + t % ncb)

        def out_map(*ids, ncb=ncb, last=last):
            t = jnp.minimum(step_of(*ids), last)
            return (t // ncb, t % ncb)

        side_in.append(pl.BlockSpec((rb, cast.block_cols), in_map))
        side_out.append(pl.BlockSpec((rb, cast.block_cols), out_map))
        side_shape.append(jax.ShapeDtypeStruct((rows, cast.width), BF16))

    def kernel(*refs):
        ins = refs[:n_in]
        cast_src = refs[n_in:n_in + n_side]
        outs = refs[n_in + n_side:n_in + n_side + n_out]
        cast_dst = refs[n_in + n_side + n_out:n_in + 2 * n_side + n_out]
        scratch = refs[n_in + 2 * n_side + n_out:]
        for src_ref, dst_ref in zip(cast_src, cast_dst):
            dst_ref[...] = src_ref[...].astype(dst_ref.dtype)
        body(*ins, *outs, *scratch)

    return pl.pallas_call(
        kernel,
        grid=grid,
        in_specs=list(in_specs) + side_in,
        out_specs=out_specs + side_out,
        out_shape=out_shape + side_shape,
        scratch_shapes=list(scratch_shapes),
        compiler_params=_params(*(("arbitrary",) * len(grid))),
        name=name,
    )(*args, *[cast.w for cast in side])


def _rmsnorm_kernel(x_ref, g_ref, o_ref):
    x = x_ref[...]
    ms = jnp.mean(x * x, axis=-1, keepdims=True)
    o_ref[...] = (x * lax.rsqrt(ms + RMS_EPS) * g_ref[...]).astype(o_ref.dtype)


def _rmsnorm(x, g):
    s, d = x.shape
    tm = _tile(s, 256)
    return pl.pallas_call(
        _rmsnorm_kernel,
        grid=(s // tm,),
        in_specs=[pl.BlockSpec((tm, d), lambda i: (i, 0)),
                  pl.BlockSpec((1, d), lambda i: (0, 0))],
        out_specs=pl.BlockSpec((tm, d), lambda i: (i, 0)),
        out_shape=jax.ShapeDtypeStruct((s, d), BF16),
        compiler_params=_params("arbitrary"),
        name="rmsnorm",
    )(x, g.reshape(1, d))


def _rope_table_kernel(pos_ref, invf_ref, sign_ref, cos_ref, sin_ref):
    ang = pos_ref[...].astype(F32) * invf_ref[...]
    cos_ref[...] = jnp.cos(ang)
    sin_ref[...] = jnp.sin(ang) * sign_ref[...]


def _rope_tables(positions):
    s = positions.shape[0]
    half = HEAD_DIM // 2
    inv_freq = ROPE_THETA ** (-jnp.arange(half, dtype=F32) / half)
    invf = jnp.concatenate([inv_freq, inv_freq]).reshape(1, HEAD_DIM)
    sign = jnp.concatenate([-jnp.ones((half,), F32),
                            jnp.ones((half,), F32)]).reshape(1, HEAD_DIM)
    tm = _tile(s, 1024)
    row = pl.BlockSpec((1, HEAD_DIM), lambda i: (0, 0))
    tab = pl.BlockSpec((tm, HEAD_DIM), lambda i: (i, 0))
    return pl.pallas_call(
        _rope_table_kernel,
        grid=(s // tm,),
        in_specs=[pl.BlockSpec((tm, 1), lambda i: (i, 0)), row, row],
        out_specs=[tab, tab],
        out_shape=[jax.ShapeDtypeStruct((s, HEAD_DIM), F32)] * 2,
        compiler_params=_params("arbitrary"),
        name="rope_tables",
    )(positions.reshape(s, 1), invf, sign)


def _qk_proj_kernel(h_ref, w_ref, g_ref, cos_ref, sin_ref, o_ref):
    tm = h_ref.shape[0]
    chunk = tm // EPILOGUE_CHUNKS
    w = w_ref[...]
    accs = [_dot(h_ref[c * chunk:(c + 1) * chunk, :], w)
            for c in range(EPILOGUE_CHUNKS)]
    for c, acc in enumerate(accs):
        rows = slice(c * chunk, (c + 1) * chunk)
        cos = cos_ref[rows, :]
        sin = sin_ref[rows, :]
        for hd in range(acc.shape[1] // HEAD_DIM):
            cols = slice(hd * HEAD_DIM, (hd + 1) * HEAD_DIM)
            a = acc[:, cols]
            ms = jnp.mean(a * a, axis=-1, keepdims=True)
            y = a * lax.rsqrt(ms + RMS_EPS) * g_ref[:, cols]
            r = y * cos + pltpu.roll(y, HEAD_DIM // 2, axis=1) * sin
            o_ref[rows, cols] = r.astype(o_ref.dtype)


def _row_chunks(ref):
    chunk = ref.shape[0] // EPILOGUE_CHUNKS
    return [slice(c * chunk, (c + 1) * chunk) for c in range(EPILOGUE_CHUNKS)]


def _plain_proj_kernel(h_ref, w_ref, o_ref, *, sigmoid):
    w = w_ref[...]
    chunks = _row_chunks(h_ref)
    accs = [_dot(h_ref[rows, :], w) for rows in chunks]
    for rows, acc in zip(chunks, accs):
        if sigmoid:
            acc = jax.nn.sigmoid(acc)
        o_ref[rows, :] = acc.astype(o_ref.dtype)


def _qk_proj(h, w_qk, gains, cos, sin, side=()):
    s, d = h.shape
    n = 2 * D_ATTN
    tm, tn = _tile(s, 1024), 1024
    tab = pl.BlockSpec((tm, HEAD_DIM), lambda j, i: (i, 0))
    return _hosted_call(
        _qk_proj_kernel,
        grid=(n // tn, s // tm),
        in_specs=[pl.BlockSpec((tm, d), lambda j, i: (i, 0)),
                  pl.BlockSpec((d, tn), lambda j, i: (0, j)),
                  pl.BlockSpec((1, tn), lambda j, i: (0, j)),
                  tab, tab],
        out_specs=pl.BlockSpec((tm, tn), lambda j, i: (i, j)),
        out_shape=jax.ShapeDtypeStruct((s, n), BF16),
        args=(h, w_qk, gains, cos, sin), name="qk_proj", side=side)


def _plain_proj(h, w, *, sigmoid, side=()):
    s, d = h.shape
    n = w.shape[1]
    tm, tn = _tile(s, 1024), 1024
    return _hosted_call(
        functools.partial(_plain_proj_kernel, sigmoid=sigmoid),
        grid=(n // tn, s // tm),
        in_specs=[pl.BlockSpec((tm, d), lambda j, i: (i, 0)),
                  pl.BlockSpec((d, tn), lambda j, i: (0, j))],
        out_specs=pl.BlockSpec((tm, tn), lambda j, i: (i, j)),
        out_shape=jax.ShapeDtypeStruct((s, n), BF16),
        args=(h, w), name="gate_proj" if sigmoid else "v_proj", side=side)


def _causal_conv3(m, w, carry_ref):
    tm = m.shape[0]
    rows = lax.broadcasted_iota(jnp.int32, m.shape, 0)
    prev = carry_ref[...]
    c2 = prev[6:7]
    c1 = prev[7:8]
    m1 = jnp.where(rows == 0, c1, pltpu.roll(m, 1, axis=0))
    m2 = jnp.where(rows == 0, c2,
                   jnp.where(rows == 1, c1, pltpu.roll(m, 2, axis=0)))
    carry_ref[...] = m[tm - 8:tm]
    return w[2:3] * m + w[0:1] * m2 + w[1:2] * m1


def _conv_proj_kernel(h_ref, wb_ref, wc_ref, wx_ref, cw_ref, o_ref, carry_ref):
    @pl.when(pl.program_id(1) == 0)
    def _():
        carry_ref[...] = jnp.zeros_like(carry_ref)

    h = h_ref[...]
    cb = _dot(h, wb_ref[...])
    m = _dot(h, wc_ref[...]) * _dot(h, wx_ref[...])
    y = _causal_conv3(m, cw_ref[...], carry_ref)
    o_ref[...] = (cb * y).astype(o_ref.dtype)


def _conv_proj(h, w, conv_w, side=()):
    s, d = h.shape
    tm, tn = _tile(s, 1024), 256
    jc, jx = D_CONV // tn, 2 * D_CONV // tn
    return _hosted_call(
        _conv_proj_kernel,
        grid=(D_CONV // tn, s // tm),
        in_specs=[pl.BlockSpec((tm, d), lambda j, i: (i, 0)),
                  pl.BlockSpec((d, tn), lambda j, i: (0, j)),
                  pl.BlockSpec((d, tn), lambda j, i: (0, jc + j)),
                  pl.BlockSpec((d, tn), lambda j, i: (0, jx + j)),
                  pl.BlockSpec((3, tn), lambda j, i: (0, j))],
        out_specs=pl.BlockSpec((tm, tn), lambda j, i: (i, j)),
        out_shape=jax.ShapeDtypeStruct((s, D_CONV), BF16),
        scratch_shapes=[pltpu.VMEM((8, tn), F32)],
        args=(h, w, w, w, conv_w), name="conv_proj", side=side)


TILE_BLOCKS = 2
ATTN_TILE = TILE_BLOCKS * MOBA_BLOCK
KEY_TILES_PER_Q = 2
KEY_BLOCKS = TILE_BLOCKS * KEY_TILES_PER_Q
KEY_TILE = KEY_BLOCKS * MOBA_BLOCK
BF16_SUBLANES = 16
ONES_ROWS = BF16_SUBLANES
SCORES_AHEAD = 2


def _attn_kernel(qlo_ref, qhi_ref, k_ref, vt_ref, olo_ref, ohi_ref,
                 kaug_ref, qaug_ref, kmean_ref, acc_ref, m_ref, *, n_blocks):
    hd = pl.program_id(0)
    step = pl.program_id(1)
    blk, d, tile = MOBA_BLOCK, HEAD_DIM, ATTN_TILE
    n_q = n_blocks // TILE_BLOCKS
    gate_rows = kmean_ref.shape[0]
    q_tiles = (step, n_q - 1 - step)

    @pl.when((hd == 0) & (step == 0))
    def _():
        kmean_ref[...] = jnp.zeros_like(kmean_ref)
        lane = lax.broadcasted_iota(jnp.int32, (blk, d), 1)

        def body(b, carry):
            rows = pl.ds(pl.multiple_of(b * blk, blk), blk)
            kaug_ref[rows, d:2 * d] = (lane == b).astype(BF16)
            return carry
        lax.fori_loop(0, n_blocks, body, 0)
        if gate_rows < d:
            for w in range(2):
                qaug_ref[w, d + gate_rows:2 * d, :] = jnp.full(
                    (d - gate_rows, tile), MASK_VALUE, BF16)

    @pl.when(step == 0)
    def _():
        def body(b, carry):
            rows = pl.ds(pl.multiple_of(b * blk, blk), blk)
            kb = k_ref[rows, :]
            kaug_ref[rows, 0:d] = kb
            kmean_ref[pl.ds(b, 1), :] = (
                jnp.sum(kb.astype(F32), axis=0, keepdims=True) * (1.0 / blk))
            return carry
        lax.fori_loop(0, n_blocks, body, 0)

    blk_id = lax.broadcasted_iota(jnp.int32, (gate_rows, tile), 0)
    q_lane = lax.broadcasted_iota(jnp.int32, (gate_rows, tile), 1)
    kmean = kmean_ref[...].astype(BF16)
    for w, (q_ref, q_tile) in enumerate(zip((qlo_ref, qhi_ref), q_tiles)):
        qt = q_ref[...]
        own_blk = q_tile * TILE_BLOCKS + q_lane // blk
        gate = _dot(kmean, qt)
        gate = jnp.where(blk_id < own_blk, gate, -jnp.inf)
        sel = blk_id == own_blk
        for _ in range(MOBA_TOPK):
            mx = jnp.max(gate, axis=0, keepdims=True)
            idx = jnp.min(jnp.where(gate == mx, blk_id, gate_rows), axis=0,
                          keepdims=True)
            hit = blk_id == idx
            sel = sel | (hit & (mx > -jnp.inf))
            gate = jnp.where(hit, -jnp.inf, gate)
        qaug_ref[w, 0:d, :] = qt
        qaug_ref[w, d:d + gate_rows, :] = jnp.where(sel, 0.0, MASK_VALUE).astype(BF16)

    def scores(t, w):
        keys = pl.ds(pl.multiple_of(t * KEY_TILE, KEY_TILE), KEY_TILE)
        return _dot(kaug_ref[keys, :], qaug_ref[w])

    def weighted_values(t, pb):
        out = _dot(vt_ref[0, t * KEY_BLOCKS], pb[0:blk])
        for u in range(1, KEY_BLOCKS):
            out += _dot(vt_ref[0, t * KEY_BLOCKS + u], pb[u * blk:(u + 1) * blk])
        return out

    n_lo = q_tiles[0] // KEY_TILES_PER_Q
    tasks = [(q_tile // KEY_TILES_PER_Q, w) for w, q_tile in enumerate(q_tiles)]
    for slot in range((n_q - 1) // KEY_TILES_PER_Q):
        tasks.append((jnp.where(slot < n_lo, slot, slot - n_lo),
                      (slot >= n_lo).astype(jnp.int32)))
    key_row = lax.broadcasted_iota(jnp.int32, (KEY_TILE, tile), 0)
    q_col = lax.broadcasted_iota(jnp.int32, (KEY_TILE, tile), 1)

    queue = [scores(*tasks[n]) for n in range(SCORES_AHEAD)]
    for n, (t, w) in enumerate(tasks):
        s = queue.pop(0)
        if n + SCORES_AHEAD < len(tasks):
            queue.append(scores(*tasks[n + SCORES_AHEAD]))
        if n < 2:
            causal = key_row + t * KEY_TILE <= q_col + q_tiles[n] * tile
            s = jnp.where(causal, s, MASK_VALUE)
            m_next = jnp.max(s, axis=0, keepdims=True)
        else:
            m_prev = m_ref[w, 0:1, :]
            m_next = jnp.maximum(m_prev, jnp.max(s, axis=0, keepdims=True))
        pv = weighted_values(t, jnp.exp2(s - m_next).astype(BF16))
        m_ref[w] = jnp.broadcast_to(m_next, m_ref.shape[1:])
        if n < 2:
            acc_ref[w] = pv
        else:
            acc_ref[w] = acc_ref[w] * jnp.exp2(m_prev - m_next) + pv

    for w, o_ref in enumerate((olo_ref, ohi_ref)):
        acc = acc_ref[w]
        o_ref[...] = (acc[0:d] / acc[d:d + 1]).T.astype(o_ref.dtype)


def _moba_attention(qk, v):
    s = v.shape[0]
    n_blocks = s // MOBA_BLOCK
    blk, d, tile = MOBA_BLOCK, HEAD_DIM, ATTN_TILE
    n_q = s // tile
    assert s % (2 * KEY_TILE) == 0 and n_blocks <= LANES
    gate_rows = -(-n_blocks // BF16_SUBLANES) * BF16_SUBLANES
    qt = qk[:, :D_ATTN].T
    vt = v.reshape(n_blocks, blk, N_ATTN_HEADS, d).transpose(2, 0, 3, 1)
    vt = jnp.concatenate(
        [vt, jnp.ones((N_ATTN_HEADS, n_blocks, ONES_ROWS, blk), BF16)], axis=2)
    half = jax.ShapeDtypeStruct((s // 2, D_ATTN), BF16)
    lo, hi = pl.pallas_call(
        functools.partial(_attn_kernel, n_blocks=n_blocks),
        grid=(N_ATTN_HEADS, n_q // 2),
        in_specs=[pl.BlockSpec((d, tile), lambda h, i: (h, i)),
                  pl.BlockSpec((d, tile), lambda h, i: (h, n_q - 1 - i)),
                  pl.BlockSpec((s, d), lambda h, i: (0, N_ATTN_HEADS + h)),
                  pl.BlockSpec((1, n_blocks, d + ONES_ROWS, blk),
                               lambda h, i: (h, 0, 0, 0))],
        out_specs=[pl.BlockSpec((tile, d), lambda h, i: (i, h)),
                   pl.BlockSpec((tile, d), lambda h, i: (n_q // 2 - 1 - i, h))],
        out_shape=[half, half],
        scratch_shapes=[pltpu.VMEM((s, 2 * d), BF16),
                        pltpu.VMEM((2, 2 * d, tile), BF16),
                        pltpu.VMEM((gate_rows, d), F32),
                        pltpu.VMEM((2, d + ONES_ROWS, tile), F32),
                        pltpu.VMEM((2, 8, tile), F32)],
        compiler_params=_params("arbitrary", "arbitrary"),
        name="moba_attention",
    )(qt, qt, qk, vt)
    return jnp.concatenate([lo, hi], axis=0)


def _merge_kernel(a_ref, c_ref, wa_ref, wc_ref, ga_ref, gc_ref, o_ref):
    ya = _dot(a_ref[...], wa_ref[...])
    yc = _dot(c_ref[...], wc_ref[...])
    o_ref[...] = (ga_ref[...].astype(F32) * ya
                  + gc_ref[...].astype(F32) * yc).astype(o_ref.dtype)


def _merge(attn, conv, w_a, w_c, gates_a, gates_c, side=()):
    s = attn.shape[0]
    n = w_a.shape[1]
    tm, tn = _tile(s, 1024), 1024
    return _hosted_call(
        _merge_kernel,
        grid=(n // tn, s // tm),
        in_specs=[pl.BlockSpec((tm, D_ATTN), lambda j, i: (i, 0)),
                  pl.BlockSpec((tm, D_CONV), lambda j, i: (i, 0)),
                  pl.BlockSpec((D_ATTN, tn), lambda j, i: (0, j)),
                  pl.BlockSpec((D_CONV, tn), lambda j, i: (0, j)),
                  pl.BlockSpec((tm, tn), lambda j, i: (i, j)),
                  pl.BlockSpec((tm, tn), lambda j, i: (i, j))],
        out_specs=pl.BlockSpec((tm, tn), lambda j, i: (i, j)),
        out_shape=jax.ShapeDtypeStruct((s, n), BF16),
        args=(attn, conv, w_a, w_c, gates_a, gates_c), name="merge", side=side)


def _residual_proj_kernel(a_ref, w_ref, x_ref, o_ref):
    o_ref[...] = x_ref[...] + _dot(a_ref[...], w_ref[...])


def _residual_proj(a, w, x):
    s, k = a.shape
    n = w.shape[1]
    tm, tn = _tile(s, 1024), 1024
    return pl.pallas_call(
        _residual_proj_kernel,
        grid=(n // tn, s // tm),
        in_specs=[pl.BlockSpec((tm, k), lambda j, i: (i, 0)),
                  pl.BlockSpec((k, tn), lambda j, i: (0, j)),
                  pl.BlockSpec((tm, tn), lambda j, i: (i, j))],
        out_specs=pl.BlockSpec((tm, tn), lambda j, i: (i, j)),
        out_shape=jax.ShapeDtypeStruct((s, n), F32),
        compiler_params=_params("arbitrary", "arbitrary"),
        name="out_proj",
    )(a, w, x)


FFN_UP_COLS = 512
FFN_UP_HALF = FFN_UP_COLS // 2


def _ffn_up_kernel(h_ref, wg_ref, wva_ref, wvb_ref, cwg_ref, cwva_ref, cwvb_ref,
                   o_ref, cg_ref, cv_ref):
    j = pl.program_id(1)

    @pl.when(pl.program_id(0) == 0)
    def _():
        cg_ref[j] = jnp.zeros(cg_ref.shape[1:], F32)
        cv_ref[j] = jnp.zeros(cv_ref.shape[1:], F32)

    h = h_ref[...]
    ug = _causal_conv3(_dot(h, wg_ref[...]), cwg_ref[...], cg_ref.at[j])
    uv = jnp.concatenate([_dot(h, wva_ref[...]), _dot(h, wvb_ref[...])], axis=1)
    cwv = jnp.concatenate([cwva_ref[...], cwvb_ref[...]], axis=1)
    uv = _causal_conv3(uv, cwv, cv_ref.at[j])
    o_ref[...] = (ug * jax.nn.sigmoid(ug) * uv).astype(o_ref.dtype)


def _ffn_up(h, wg, wv, conv_w, side=()):
    s, d = h.shape
    f = wv.shape[1]
    tm, tn, th = _tile(s, 1024), FFN_UP_COLS, FFN_UP_HALF
    nj = pl.cdiv(f, tn)
    assert f % th == 0 and wg.shape[1] == nj * tn
    v0 = f // th
    last = f // th - 1

    def g_map(i, j):
        return (0, j)

    def va_map(i, j):
        return (0, 2 * j)

    def vb_map(i, j):
        return (0, jnp.minimum(2 * j + 1, last))

    return _hosted_call(
        _ffn_up_kernel,
        grid=(s // tm, nj),
        in_specs=[pl.BlockSpec((tm, d), lambda i, j: (i, 0)),
                  pl.BlockSpec((d, tn), g_map),
                  pl.BlockSpec((d, th), va_map),
                  pl.BlockSpec((d, th), vb_map),
                  pl.BlockSpec((3, tn), g_map),
                  pl.BlockSpec((3, th), lambda i, j: (0, v0 + 2 * j)),
                  pl.BlockSpec((3, th), lambda i, j: (0, v0 + jnp.minimum(2 * j + 1, last)))],
        out_specs=pl.BlockSpec((tm, tn), lambda i, j: (i, j)),
        out_shape=jax.ShapeDtypeStruct((s, f), BF16),
        scratch_shapes=[pltpu.VMEM((nj, 8, tn), F32),
                        pltpu.VMEM((nj, 8, tn), F32)],
        args=(h, wg, wv, wv, conv_w, conv_w, conv_w), name="ffn_up", side=side)


def _ffn_down_kernel(a_ref, w_ref, x_ref, o_ref, ob_ref):
    out = x_ref[...] + _dot(a_ref[...], w_ref[...])
    o_ref[...] = out
    ob_ref[...] = out.astype(ob_ref.dtype)


def _ffn_down(act, w, x):
    s, f = act.shape
    n = w.shape[1]
    tm, tn = _tile(s, 512), 512
    out = pl.BlockSpec((tm, tn), lambda j, i: (i, j))
    return pl.pallas_call(
        _ffn_down_kernel,
        grid=(n // tn, s // tm),
        in_specs=[pl.BlockSpec((tm, f), lambda j, i: (i, 0)),
                  pl.BlockSpec((f, tn), lambda j, i: (0, j)),
                  out],
        out_specs=[out, out],
        out_shape=[jax.ShapeDtypeStruct((s, n), F32),
                   jax.ShapeDtypeStruct((s, n), BF16)],
        compiler_params=_params("arbitrary", "arbitrary"),
        name="ffn_down",
    )(act, w, x)


def _ple_kernel(xb_ref, wpg_ref, p_ref, wp_ref, x_ref, o_ref):
    wpg, wp = wpg_ref[...], wp_ref[...]
    chunks = _row_chunks(xb_ref)
    raw = [(_dot(xb_ref[rows, :], wpg), _dot(p_ref[rows, :].astype(BF16), wp))
           for rows in chunks]
    for rows, (gate, emb) in zip(chunks, raw):
        o_ref[rows, :] = x_ref[rows, :] + jax.nn.sigmoid(gate) * emb


def _ple(xb, w_pg, p, w_p, x):
    s, d = xb.shape
    n = w_pg.shape[1]
    dp = p.shape[1]
    tm, tn = _tile(s, 1024), 512
    return pl.pallas_call(
        _ple_kernel,
        grid=(s // tm, n // tn),
        in_specs=[pl.BlockSpec((tm, d), lambda i, j: (i, 0)),
                  pl.BlockSpec((d, tn), lambda i, j: (0, j)),
                  pl.BlockSpec((tm, dp), lambda i, j: (i, 0)),
                  pl.BlockSpec((dp, tn), lambda i, j: (0, j)),
                  pl.BlockSpec((tm, tn), lambda i, j: (i, j))],
        out_specs=pl.BlockSpec((tm, tn), lambda i, j: (i, j)),
        out_shape=jax.ShapeDtypeStruct((s, n), F32),
        compiler_params=_params("arbitrary", "arbitrary"),
        name="ple",
    )(xb, w_pg, p, w_p, x)


def _layer(x, p, cos, sin, w_norm_mix, w_in, q_norm, k_norm, conv_mix_w,
           w_attn_branch, w_conv_branch, w_out, w_norm_ffn, w_up, ffn_conv_w,
           w_down, w_ple, w_ple_gate):
    d_model = x.shape[1]
    v0, c0, g0 = 2 * D_ATTN, 3 * D_ATTN, 3 * D_ATTN + 3 * D_CONV

    gains = jnp.concatenate([jnp.tile(q_norm * (HEAD_DIM ** -0.5 * LOG2_E), N_ATTN_HEADS),
                             jnp.tile(k_norm, N_ATTN_HEADS)]).reshape(1, 2 * D_ATTN)

    d_ff = w_down.shape[0]
    g_cols = pl.cdiv(d_ff, FFN_UP_COLS) * FFN_UP_COLS
    h = _rmsnorm(x, w_norm_mix)
    qk, w_v, w_conv, w_ga = _qk_proj(
        h, w_in[:, :v0].astype(BF16), gains, cos, sin,
        side=(SideCast(w_in, v0, D_ATTN, D_ATTN),
              SideCast(w_in, c0, 3 * D_CONV, 3 * D_CONV),
              SideCast(w_in, g0, d_model, d_model)))
    v, w_a, w_c, w_gc = _plain_proj(
        h, w_v, sigmoid=False,
        side=(_whole(w_attn_branch), _whole(w_conv_branch),
              SideCast(w_in, g0 + d_model, d_model, d_model)))
    conv, = _conv_proj(h, w_conv, conv_mix_w)
    gates_a, w_up_g, w_out_b = _plain_proj(
        h, w_ga, sigmoid=True,
        side=(SideCast(w_up, 0, g_cols, g_cols), _whole(w_out)))
    gates_c, w_up_v = _plain_proj(h, w_gc, sigmoid=True,
                                  side=(SideCast(w_up, d_ff, d_ff, d_ff),))
    attn = _moba_attention(qk, v)
    merged, w_pg = _merge(attn, conv, w_a, w_c, gates_a, gates_c,
                          side=(_whole(w_ple_gate),))
    x1 = _residual_proj(merged, w_out_b, x)

    h2 = _rmsnorm(x1, w_norm_ffn)
    act, w_down_b = _ffn_up(h2, w_up_g, w_up_v, ffn_conv_w, side=(_whole(w_down),))
    x2, x2b = _ffn_down(act, w_down_b, x1)

    return _ple(x2b, w_pg, p, w_ple.astype(BF16), x2)


def kernel(x, p, positions, w_norm_mix, w_in, q_norm, k_norm, conv_mix_w,
           w_attn_branch, w_conv_branch, w_out, w_norm_ffn, w_up, ffn_conv_w,
           w_down, w_ple, w_ple_gate):
    b, s, d = x.shape
    assert b == 1, "single-sequence prefill only"
    xs = x.reshape(s, d)
    cos, sin = _rope_tables(positions.reshape(s))
    for i in range(w_in.shape[0]):
        xs = _layer(xs, p[i].reshape(s, -1), cos, sin, w_norm_mix[i], w_in[i],
                    q_norm[i], k_norm[i], conv_mix_w[i], w_attn_branch[i],
                    w_conv_branch[i], w_out[i], w_norm_ffn[i], w_up[i],
                    ffn_conv_w[i], w_down[i], w_ple[i], w_ple_gate[i])
    return xs.reshape(b, s, d)
```
